```python
import jax, jax.numpy as jnp
from jax import lax
import numpy as np

D_MODEL = 2048
BATCH = 4
SEQ = 2048
DEPTH = 4
DEC_BATCH = 32
DEC_SEQ = 8
PAST_LEN = 16384
PAGE_SIZE = 128

N_MIXERS = 3
N_HEADS = 16
HEAD_DIM = D_MODEL // N_HEADS
D_FF = 4 * D_MODEL
ROPE_THETA = 10000.0
NORM_EPS = 1e-6
Q_BLOCK = 128
NSA_KV_HEADS = 2
NSA_CMP_BLOCK = 32
NSA_SEL_BLOCK = 64
NSA_N_SEL = 16
NSA_WINDOW = 512
NSA_STREAMS = 4
NSA_FORCED_BONUS = 1e4
SWA_KV_HEADS = 4
SWA_WINDOW = 128
DSA_KV_HEADS = 4
DSA_IDX_HEADS = 16
DSA_IDX_DIM = 64
DSA_TOPK = 256

N_A_LAYERS = (DEPTH + 2) // 3
N_B_LAYERS = (DEPTH + 1) // 3
N_C_LAYERS = DEPTH // 3
NSA_IN = N_HEADS * HEAD_DIM + 6 * NSA_KV_HEADS * HEAD_DIM + 3 * N_HEADS
SWA_IN = N_HEADS * HEAD_DIM + 2 * SWA_KV_HEADS * HEAD_DIM
DSA_IN = N_HEADS * HEAD_DIM + 2 * DSA_KV_HEADS * HEAD_DIM + DSA_IDX_HEADS * DSA_IDX_DIM + DSA_IDX_DIM + DSA_IDX_HEADS

kernel_name = 'nsa_swa_dsa_hybrid_decode_step'


def rms_norm(x, g):
    xf = x.astype(jnp.float32)
    y = xf * lax.rsqrt(jnp.mean(xf * xf, axis=-1, keepdims=True) + NORM_EPS)
    return (y * g.astype(jnp.float32)).astype(x.dtype)


def rope(x, pos):
    d = x.shape[-1]
    inv = ROPE_THETA ** (-jnp.arange(0, d, 2, dtype=jnp.float32) / d)
    ang = pos.astype(jnp.float32)[:, None] * inv[None, :]
    cos, sin = jnp.cos(ang)[:, None, :], jnp.sin(ang)[:, None, :]
    xf = x.astype(jnp.float32)
    x1, x2 = xf[..., : d // 2], xf[..., d // 2:]
    return jnp.concatenate([x1 * cos - x2 * sin, x2 * cos + x1 * sin], axis=-1).astype(x.dtype)


def gqa_attend(q, k, v, valid, sink=None):
    *lead, tq, n_h, dh = q.shape
    n_g = k.shape[-2]
    qg = q.reshape(*lead, tq, n_g, n_h // n_g, dh)
    s = jnp.einsum('...qgrd,...kgd->...grqk', qg, k, preferred_element_type=jnp.float32) * (dh ** -0.5)
    s = jnp.where(valid[..., None, None, :, :], s, -jnp.inf)
    m = jnp.max(s, axis=-1, keepdims=True)
    if sink is not None:
        sk = sink.astype(jnp.float32).reshape(n_g, n_h // n_g, 1, 1)
        m = jnp.maximum(m, sk)
    m = jnp.where(jnp.isfinite(m), m, 0.0)
    e = jnp.exp(s - m)
    den = jnp.sum(e, axis=-1, keepdims=True)
    if sink is not None:
        den = den + jnp.exp(sk - m)
    p = e / jnp.maximum(den, 1e-30)
    o = jnp.einsum('...grqk,...kgd->...qgrd', p.astype(v.dtype), v)
    return o.reshape(*lead, tq, n_h, dh), p


def sweep_query_blocks(fn, q_pos, *arrs):
    t = q_pos.shape[0]
    qb = Q_BLOCK if t % Q_BLOCK == 0 else t
    nb = t // qb
    pos_b = q_pos.reshape(nb, qb)
    arr_b = [jnp.moveaxis(a.reshape(a.shape[0], nb, qb, *a.shape[2:]), 1, 0) for a in arrs]
    outs = lax.map(lambda args: fn(args[0], *args[1:]), (pos_b, *arr_b))
    return tuple(jnp.moveaxis(o, 0, 1).reshape(o.shape[1], t, *o.shape[3:]) for o in outs)


def gather_pages(cache, layer, page_table, stream=None):
    b, n_pages = page_table.shape
    page = cache.shape[2]
    idx = (layer, page_table[:, :, None], jnp.arange(page)[None, None, :])
    if stream is not None:
        idx = idx + (stream,)
    rows = cache[idx]
    return rows.reshape(b, n_pages * page, *rows.shape[3:])


def band_keys(a, window):
    t = a.shape[1]
    nblk = t // Q_BLOCK
    n_prev = -(-window // Q_BLOCK)
    pad = n_prev * Q_BLOCK
    kb = pad + Q_BLOCK
    a_p = jnp.pad(a, [(0, 0), (pad, 0)] + [(0, 0)] * (a.ndim - 2))
    idx = jnp.arange(nblk)[:, None] * Q_BLOCK + jnp.arange(kb)[None, :]
    return a_p[:, idx], idx - pad


def window_attention(q, k, v, q_pos, k_pos, window, sink):
    rel = q_pos[..., :, None] - k_pos[..., None, :]
    valid = (rel >= 0) & (rel < window) & (k_pos[..., None, :] >= 0)
    return gqa_attend(q, k, v, valid, sink)[0]


def sliding_attention(q, k, v, pos, buf_k, buf_v, window, sink):
    if buf_k is None:
        b, t, n_h, dh = q.shape
        nblk = t // Q_BLOCK
        k_band, k_pos = band_keys(k, window)
        v_band, _ = band_keys(v, window)
        o = window_attention(q.reshape(b, nblk, Q_BLOCK, n_h, dh), k_band, v_band,
                             pos.reshape(nblk, Q_BLOCK), k_pos, window, sink)
        return o.reshape(b, t, n_h, dh)
    w = buf_k.shape[1]
    k_pos = jnp.concatenate([pos[0] - w + jnp.arange(w, dtype=pos.dtype), pos])
    return window_attention(q, jnp.concatenate([buf_k, k], 1), jnp.concatenate([buf_v, v], 1),
                            pos, k_pos, window, sink)


def nsa_mixer(h, pos, w_in, w_out, cmp_pos_k, cmp_pos_v, cmp_wk, cmp_wv, past):
    b, t, _ = h.shape
    n_h, n_g, dh = N_HEADS, NSA_KV_HEADS, HEAD_DIM
    proj = h @ w_in
    q_end = n_h * dh
    kv_end = q_end + 6 * n_g * dh
    q = proj[..., :q_end].reshape(b, t, n_h, dh)
    kv = proj[..., q_end:kv_end].reshape(b, t, 6, n_g, dh)
    gates = jax.nn.sigmoid(proj[..., kv_end:].astype(jnp.float32)).reshape(b, t, 3, n_h).astype(h.dtype)
    k_cmp, v_cmp, k_sel, v_sel, k_win, v_win = [kv[:, :, s] for s in range(6)]
    q_rot = rope(q, pos)
    k_sel = rope(k_sel, pos)
    k_win = rope(k_win, pos)
    new_rows = jnp.stack([k_cmp, v_cmp, k_sel, v_sel], axis=2)
    win_rows = jnp.stack([k_win, v_win], axis=2)
    if past is None:
        kc_all, vc_all, ks_all, vs_all = k_cmp, v_cmp, k_sel, v_sel
        buf_k = buf_v = None
    else:
        cache, layer, page_table, win_buf = past
        kc_all, vc_all, ks_all, vs_all = [
            jnp.concatenate([gather_pages(cache, layer, page_table, s), new], axis=1)
            for s, new in enumerate((k_cmp, v_cmp, k_sel, v_sel))]
        buf_k, buf_v = win_buf[:, :, 0], win_buf[:, :, 1]
    seq_len = kc_all.shape[1]
    n_cmp = seq_len // NSA_CMP_BLOCK
    def compress(a, alpha, w):
        blocks = a[:, :n_cmp * NSA_CMP_BLOCK].reshape(b, n_cmp, NSA_CMP_BLOCK, n_g, dh)
        return jnp.einsum('bnlgd,l,de->bnge', blocks, alpha, w)
    kc = compress(kc_all, cmp_pos_k, cmp_wk)
    vc = compress(vc_all, cmp_pos_v, cmp_wv)
    cmp_end = (jnp.arange(n_cmp) + 1) * NSA_CMP_BLOCK - 1
    n_slc = -(-seq_len // NSA_SEL_BLOCK)
    n_sel = min(NSA_N_SEL, n_slc)
    per = NSA_SEL_BLOCK // NSA_CMP_BLOCK
    r = n_h // n_g
    blk = jnp.arange(n_slc)

    def block_fn(qp, qb, qrb):
        nq = qp.shape[0]
        o_c, p_c = gqa_attend(qb, kc, vc, cmp_end[None, :] <= qp[:, None])
        imp = jnp.sum(p_c, axis=2)
        imp = jnp.pad(imp, ((0, 0), (0, 0), (0, 0), (0, n_slc * per - n_cmp)))
        imp = imp.reshape(b, n_g, nq, n_slc, per).sum(-1)
        cur = qp[:, None] // NSA_SEL_BLOCK
        visible = blk[None, :] * NSA_SEL_BLOCK <= qp[:, None]
        forced = (blk[None, :] == 0) | (blk[None, :] == cur) | (blk[None, :] == cur - 1)
        score = jnp.where(visible, imp + jnp.where(forced, NSA_FORCED_BONUS, 0.0), -jnp.inf)
        score = jnp.moveaxis(score, 1, 2)
        top, sel = lax.top_k(score, n_sel)
        kp = (sel[..., None] * NSA_SEL_BLOCK + jnp.arange(NSA_SEL_BLOCK)).reshape(b, nq, n_g, n_sel * NSA_SEL_BLOCK)
        valid_s = jnp.repeat(jnp.isfinite(top), NSA_SEL_BLOCK, axis=-1) & (kp <= qp[None, :, None, None])
        kpc = jnp.minimum(kp, seq_len - 1)
        bi = jnp.arange(b)[:, None, None, None]
        gi = jnp.arange(n_g)[None, None, :, None]
        k_g = ks_all[bi, kpc, gi]
        v_g = vs_all[bi, kpc, gi]
        o_s, _ = gqa_attend(qrb.reshape(b, nq, n_g, 1, r, dh), k_g[..., None, :], v_g[..., None, :],
                            valid_s[..., None, :])
        return o_c, o_s.reshape(b, nq, n_h, dh)

    o_c, o_s = sweep_query_blocks(block_fn, pos, q, q_rot)
    o_w = sliding_attention(q_rot, k_win, v_win, pos, buf_k, buf_v, NSA_WINDOW, None)
    o = gates[:, :, 0, :, None] * o_c + gates[:, :, 1, :, None] * o_s + gates[:, :, 2, :, None] * o_w
    out = o.reshape(b, t, n_h * dh) @ w_out
    if past is None:
        win_state = win_rows[:, -min(NSA_WINDOW, t):]
    else:
        win_state = jnp.concatenate([past[3], win_rows], axis=1)[:, -past[3].shape[1]:]
    return out, new_rows, win_state


def swa_mixer(h, pos, w_in, w_out, sinks, buf):
    b, t, _ = h.shape
    n_h, n_g, dh = N_HEADS, SWA_KV_HEADS, HEAD_DIM
    proj = h @ w_in
    q = rope(proj[..., :n_h * dh].reshape(b, t, n_h, dh), pos)
    kv = proj[..., n_h * dh:].reshape(b, t, 2, n_g, dh)
    k = rope(kv[:, :, 0], pos)
    v = kv[:, :, 1]
    if buf is None:
        o = sliding_attention(q, k, v, pos, None, None, SWA_WINDOW, sinks)
    else:
        o = sliding_attention(q, k, v, pos, buf[:, :, 0], buf[:, :, 1], SWA_WINDOW, sinks)
    rows = jnp.stack([k, v], axis=2)
    if buf is None:
        state = rows[:, -min(SWA_WINDOW, t):]
    else:
        state = jnp.concatenate([buf, rows], axis=1)[:, -buf.shape[1]:]
    return o.reshape(b, t, n_h * dh) @ w_out, state


def dsa_mixer(h, pos, w_in, w_out, past):
    b, t, _ = h.shape
    n_h, n_g, dh = N_HEADS, DSA_KV_HEADS, HEAD_DIM
    o1 = n_h * dh
    o2 = o1 + n_g * dh
    o3 = o2 + n_g * dh
    o4 = o3 + DSA_IDX_HEADS * DSA_IDX_DIM
    o5 = o4 + DSA_IDX_DIM
    proj = h @ w_in
    q = rope(proj[..., :o1].reshape(b, t, n_h, dh), pos)
    k = rope(proj[..., o1:o2].reshape(b, t, n_g, dh), pos)
    v = proj[..., o2:o3].reshape(b, t, n_g, dh)
    qi = rope(proj[..., o3:o4].reshape(b, t, DSA_IDX_HEADS, DSA_IDX_DIM), pos)
    ki = rope(proj[..., o4:o5].reshape(b, t, 1, DSA_IDX_DIM), pos)[:, :, 0]
    wt = proj[..., o5:].astype(jnp.float32) * (DSA_IDX_HEADS ** -0.5)
    rows = jnp.stack([k, v], axis=2)
    if past is None:
        k_all, v_all, ki_all = k, v, ki
    else:
        cache, cache_idx, layer, page_table = past
        k_all = jnp.concatenate([gather_pages(cache, layer, page_table, 0), k], axis=1)
        v_all = jnp.concatenate([gather_pages(cache, layer, page_table, 1), v], axis=1)
        ki_all = jnp.concatenate([gather_pages(cache_idx, layer, page_table), ki], axis=1)
    seq_len = k_all.shape[1]
    n_keep = min(DSA_TOPK, seq_len // 4)
    k_pos = jnp.arange(seq_len)

    def block_fn(qp, qb, qib, wb):
        sc = jnp.einsum('bqhd,bkd->bqhk', qib, ki_all, preferred_element_type=jnp.float32) * (DSA_IDX_DIM ** -0.5)
        idx_score = jnp.einsum('bqhk,bqh->bqk', jax.nn.relu(sc), wb)
        idx_score = jnp.where(k_pos[None, None, :] <= qp[None, :, None], idx_score, -jnp.inf)
        top, sel = lax.top_k(idx_score, n_keep)
        bi = jnp.arange(b)[:, None, None]
        k_g = k_all[bi, sel]
        v_g = v_all[bi, sel]
        o, _ = gqa_attend(qb[:, :, None], k_g, v_g, jnp.isfinite(top)[:, :, None, :])
        return (o[:, :, 0],)

    (o,) = sweep_query_blocks(block_fn, pos, q, qi, wt)
    return o.reshape(b, t, n_h * dh) @ w_out, rows, ki


def sqrelu_mlp(h, w_up, w_down):
    u = jax.nn.relu(h @ w_up)
    return (u * u) @ w_down


def setup_inputs(seed: int = 0) -> dict:
    key = jax.random.key(seed)
    keys = iter(jax.random.split(key, 32))
    def nrm(shape, scale):
        return jax.random.normal(next(keys), shape, jnp.float32) * scale
    n_pages = PAST_LEN // PAGE_SIZE
    in_use = DEC_BATCH * n_pages
    pool = in_use + max(1, in_use // 4)
    page_table = jax.random.permutation(next(keys), pool)[:in_use].reshape(DEC_BATCH, n_pages).astype(jnp.int32)
    wa = min(NSA_WINDOW, PAST_LEN)
    wb = min(SWA_WINDOW, PAST_LEN)
    hd = N_HEADS * HEAD_DIM
    return {
        'x_prompt': nrm((BATCH, SEQ, D_MODEL), 1.0),
        'x_sample': nrm((DEC_BATCH, DEC_SEQ, D_MODEL), 1.0),
        'cache_nsa': nrm((N_A_LAYERS, pool, PAGE_SIZE, NSA_STREAMS, NSA_KV_HEADS, HEAD_DIM), 1.0),
        'state_nsa_win': nrm((N_A_LAYERS, DEC_BATCH, wa, 2, NSA_KV_HEADS, HEAD_DIM), 1.0),
        'state_swa': nrm((N_B_LAYERS, DEC_BATCH, wb, 2, SWA_KV_HEADS, HEAD_DIM), 1.0),
        'cache_dsa': nrm((N_C_LAYERS, pool, PAGE_SIZE, 2, DSA_KV_HEADS, HEAD_DIM), 1.0),
        'cache_dsa_idx': nrm((N_C_LAYERS, pool, PAGE_SIZE, DSA_IDX_DIM), 1.0),
        'page_table': page_table,
        'norm_mixer': 1.0 + nrm((DEPTH, D_MODEL), 0.02),
        'norm_mlp': 1.0 + nrm((DEPTH, D_MODEL), 0.02),
        'norm_final': 1.0 + nrm((D_MODEL,), 0.02),
        'w_mlp_up': nrm((DEPTH, D_MODEL, D_FF), D_MODEL ** -0.5),
        'w_mlp_down': nrm((DEPTH, D_FF, D_MODEL), D_FF ** -0.5),
        'nsa_w_in': nrm((N_A_LAYERS, D_MODEL, NSA_IN), D_MODEL ** -0.5),
        'nsa_w_out': nrm((N_A_LAYERS, hd, D_MODEL), hd ** -0.5),
        'nsa_cmp_pos_k': (1.0 + nrm((N_A_LAYERS, NSA_CMP_BLOCK), 0.1)) / NSA_CMP_BLOCK,
        'nsa_cmp_pos_v': (1.0 + nrm((N_A_LAYERS, NSA_CMP_BLOCK), 0.1)) / NSA_CMP_BLOCK,
        'nsa_cmp_wk': nrm((N_A_LAYERS, HEAD_DIM, HEAD_DIM), HEAD_DIM ** -0.5),
        'nsa_cmp_wv': nrm((N_A_LAYERS, HEAD_DIM, HEAD_DIM), HEAD_DIM ** -0.5),
        'swa_w_in': nrm((N_B_LAYERS, D_MODEL, SWA_IN), D_MODEL ** -0.5),
        'swa_w_out': nrm((N_B_LAYERS, hd, D_MODEL), hd ** -0.5),
        'swa_sinks': nrm((N_B_LAYERS, N_HEADS), 0.5),
        'dsa_w_in': nrm((N_C_LAYERS, D_MODEL, DSA_IN), D_MODEL ** -0.5),
        'dsa_w_out': nrm((N_C_LAYERS, hd, D_MODEL), hd ** -0.5),
    }


def reference(x_prompt, x_sample, cache_nsa, state_nsa_win, state_swa, cache_dsa, cache_dsa_idx, page_table,
              norm_mixer, norm_mlp, norm_final, w_mlp_up, w_mlp_down, nsa_w_in, nsa_w_out, nsa_cmp_pos_k,
              nsa_cmp_pos_v, nsa_cmp_wk, nsa_cmp_wv, swa_w_in, swa_w_out, swa_sinks, dsa_w_in, dsa_w_out):
    past_len = page_table.shape[1] * cache_nsa.shape[2]

    def run(x, pos, is_sample):
        nsa_rows, nsa_win, swa_win, dsa_rows, dsa_idx = [], [], [], [], []
        for i in range(DEPTH):
            kind, j = i % N_MIXERS, i // N_MIXERS
            h = rms_norm(x, norm_mixer[i])
            if kind == 0:
                past = (cache_nsa, j, page_table, state_nsa_win[j]) if is_sample else None
                o, rows, win = nsa_mixer(h, pos, nsa_w_in[j], nsa_w_out[j], nsa_cmp_pos_k[j], nsa_cmp_pos_v[j],
                                         nsa_cmp_wk[j], nsa_cmp_wv[j], past)
                nsa_rows.append(rows)
                nsa_win.append(win)
            elif kind == 1:
                buf = state_swa[j] if is_sample else None
                o, win = swa_mixer(h, pos, swa_w_in[j], swa_w_out[j], swa_sinks[j], buf)
                swa_win.append(win)
            else:
                past = (cache_dsa, cache_dsa_idx, j, page_table) if is_sample else None
                o, rows, idx_rows = dsa_mixer(h, pos, dsa_w_in[j], dsa_w_out[j], past)
                dsa_rows.append(rows)
                dsa_idx.append(idx_rows)
            x = x + o
            x = x + sqrelu_mlp(rms_norm(x, norm_mlp[i]), w_mlp_up[i], w_mlp_down[i])
        states = (jnp.stack(nsa_rows), jnp.stack(nsa_win), jnp.stack(swa_win), jnp.stack(dsa_rows), jnp.stack(dsa_idx))
        return rms_norm(x, norm_final), states

    y_prompt, (p_nsa, p_nsa_win, p_swa, p_dsa, p_dsa_idx) = run(
        x_prompt, jnp.arange(x_prompt.shape[1], dtype=jnp.int32), False)
    y_sample, (s_nsa, s_nsa_win, s_swa, s_dsa, s_dsa_idx) = run(
        x_sample, past_len + jnp.arange(x_sample.shape[1], dtype=jnp.int32), True)
    return (y_prompt, y_sample, p_nsa, s_nsa, p_nsa_win, s_nsa_win, p_swa, s_swa, p_dsa, s_dsa, p_dsa_idx, s_dsa_idx)
```

```python
import functools

import jax
import jax.numpy as jnp
from jax import lax
from jax.experimental import pallas as pl
from jax.experimental.pallas import tpu as pltpu

F32 = jnp.float32
BF16 = jnp.bfloat16
I32 = jnp.int32

ROPE_THETA = 10000.0
NORM_EPS = 1e-6
NSA_SEL_BLOCK = 64
NSA_N_SEL = 16
NSA_WINDOW = 512
NSA_FORCED_BONUS = 1e4
SWA_WINDOW = 128
DSA_TOPK = 256
Q_TILE = 128
PAGES_PER_STEP = 8

LANES = 128
MASK_NEG = -1e30
INT_MIN = -2 ** 31
MIB = 1024 * 1024


def _params(n_axes, vmem_mib=32):
    return pltpu.CompilerParams(dimension_semantics=("arbitrary",) * n_axes,
                                vmem_limit_bytes=vmem_mib * MIB)


def _dot(a, b):
    return jnp.dot(a, b, preferred_element_type=F32)


def _dot_nt(a, b):
    return lax.dot_general(a, b, (((1,), (1,)), ((), ())), preferred_element_type=F32)


def _iota(shape, dim):
    return lax.broadcasted_iota(I32, shape, dim)


def _log2(n):
    assert n > 0 and n & (n - 1) == 0, n
    return n.bit_length() - 1


def _rmsnorm_kernel(x_ref, g_ref, o_ref):
    x = x_ref[...]
    y = x * lax.rsqrt(jnp.mean(x * x, axis=-1, keepdims=True) + NORM_EPS)
    o_ref[...] = (y * g_ref[...]).astype(o_ref.dtype)


def rmsnorm(x, g, out_dtype):
    m, d = x.shape
    tm = min(m, 256)
    return pl.pallas_call(
        _rmsnorm_kernel,
        grid=(m // tm,),
        in_specs=[pl.BlockSpec((tm, d), lambda i: (i, 0)),
                  pl.BlockSpec((1, d), lambda i: (0, 0))],
        out_specs=pl.BlockSpec((tm, d), lambda i: (i, 0)),
        out_shape=jax.ShapeDtypeStruct((m, d), out_dtype),
        compiler_params=_params(1),
        name="rmsnorm",
    )(x, g.reshape(1, d))


def _mm_kernel(*refs, nk, act, has_res):
    a_ref, w_ref = refs[0], refs[1]
    res_ref = refs[2] if has_res else None
    o_ref = refs[2 + has_res]
    acc_ref = refs[3 + has_res] if nk > 1 else None
    part = _dot(a_ref[...].astype(BF16), w_ref[...].astype(BF16))

    def finish(y):
        if act == "relu2":
            y = jnp.maximum(y, 0.0)
            y = y * y
        if has_res:
            y = y + res_ref[...]
        o_ref[...] = y.astype(o_ref.dtype)

    if nk == 1:
        finish(part)
        return
    k = pl.program_id(2)

    @pl.when(k == 0)
    def _():
        acc_ref[...] = part

    @pl.when(k > 0)
    def _():
        acc_ref[...] += part

    @pl.when(k == nk - 1)
    def _():
        finish(acc_ref[...])


def matmul(a, w, *, residual=None, act=None, out_dtype=F32, tm=512, tn=512, tk=2048):
    m, kd = a.shape
    n = w.shape[1]
    tm, tn, tk = min(tm, m), min(tn, n), min(tk, kd)
    assert m % tm == 0 and kd % tk == 0
    nk = kd // tk
    in_specs = [pl.BlockSpec((tm, tk), lambda j, i, k: (i, k)),
                pl.BlockSpec((tk, tn), lambda j, i, k: (k, j))]
    args = [a, w]
    if residual is not None:
        in_specs.append(pl.BlockSpec((tm, tn), lambda j, i, k: (i, j)))
        args.append(residual)
    return pl.pallas_call(
        functools.partial(_mm_kernel, nk=nk, act=act, has_res=residual is not None),
        grid=(pl.cdiv(n, tn), m // tm, nk),
        in_specs=in_specs,
        out_specs=pl.BlockSpec((tm, tn), lambda j, i, k: (i, j)),
        out_shape=jax.ShapeDtypeStruct((m, n), out_dtype),
        scratch_shapes=[pltpu.VMEM((tm, tn), F32)] if nk > 1 else [],
        compiler_params=_params(3, 48),
        name="matmul",
    )(*args)


def rope_tables(pos, head_dim, idx_dim):
    def angles(d):
        inv = ROPE_THETA ** (-jnp.arange(0, d, 2, dtype=F32) / d)
        ang = pos.astype(F32)[:, None] * inv[None, :]
        return jnp.cos(ang), jnp.sin(ang)

    assert head_dim == LANES and LANES % idx_dim == 0
    cos, sin = angles(head_dim)
    rep = LANES // idx_dim
    cos_i, sin_i = angles(idx_dim)
    zero = jnp.zeros_like(sin_i)
    return jnp.concatenate([
        cos, cos, -sin, sin,
        jnp.tile(jnp.concatenate([cos_i, cos_i], 1), (1, rep)),
        jnp.tile(jnp.concatenate([zero, sin_i], 1), (1, rep)),
        jnp.tile(jnp.concatenate([-sin_i, zero], 1), (1, rep)),
    ], axis=1)


def _rope_heads(src_ref, col, n, tab_ref, dst_ref, dst_col):
    cos = tab_ref[:, 0:LANES]
    sin = tab_ref[:, LANES:2 * LANES]
    for h in range(n):
        x = src_ref[:, col + h * LANES:col + (h + 1) * LANES]
        y = x * cos + pltpu.roll(x, LANES // 2, 1) * sin
        dst_ref[:, dst_col + h * LANES:dst_col + (h + 1) * LANES] = y.astype(dst_ref.dtype)


def _rope_idx(x, tab_ref, half):
    cos = tab_ref[:, 2 * LANES:3 * LANES]
    sin_p = tab_ref[:, 3 * LANES:4 * LANES]
    sin_n = tab_ref[:, 4 * LANES:5 * LANES]
    return x * cos + pltpu.roll(x, half, 1) * sin_p + pltpu.roll(x, LANES - half, 1) * sin_n


def _nsa_prep_kernel(p_ref, tab_ref, q_ref, qr_ref, rows_ref, win_ref, gate_ref, *, n_h, n_g):
    hd = n_h * LANES
    gd = n_g * LANES
    q_ref[...] = p_ref[:, 0:hd].astype(q_ref.dtype)
    _rope_heads(p_ref, 0, n_h, tab_ref, qr_ref, 0)
    rows_ref[:, 0:2 * gd] = p_ref[:, hd:hd + 2 * gd]
    _rope_heads(p_ref, hd + 2 * gd, n_g, tab_ref, rows_ref, 2 * gd)
    rows_ref[:, 3 * gd:4 * gd] = p_ref[:, hd + 3 * gd:hd + 4 * gd]
    _rope_heads(p_ref, hd + 4 * gd, n_g, tab_ref, win_ref, 0)
    win_ref[:, gd:2 * gd] = p_ref[:, hd + 5 * gd:hd + 6 * gd]
    gate_ref[...] = jax.nn.sigmoid(p_ref[:, hd + 6 * gd:hd + 6 * gd + 3 * n_h])


def nsa_prep(proj, tab, n_h, n_g, q_dtype):
    m, n_in = proj.shape
    tq = min(m, 256)
    hd, gd = n_h * LANES, n_g * LANES
    row = lambda w: pl.BlockSpec((tq, w), lambda i: (i, 0))
    return pl.pallas_call(
        functools.partial(_nsa_prep_kernel, n_h=n_h, n_g=n_g),
        grid=(m // tq,),
        in_specs=[row(n_in), row(5 * LANES)],
        out_specs=[row(hd), row(hd), row(4 * gd), row(2 * gd), row(3 * n_h)],
        out_shape=[jax.ShapeDtypeStruct((m, hd), q_dtype), jax.ShapeDtypeStruct((m, hd), q_dtype),
                   jax.ShapeDtypeStruct((m, 4 * gd), F32), jax.ShapeDtypeStruct((m, 2 * gd), F32),
                   jax.ShapeDtypeStruct((m, 3 * n_h), F32)],
        compiler_params=_params(1),
        name="nsa_prep",
    )(proj, tab)


def _swa_prep_kernel(p_ref, tab_ref, qr_ref, rows_ref, *, n_h, n_g):
    hd, gd = n_h * LANES, n_g * LANES
    _rope_heads(p_ref, 0, n_h, tab_ref, qr_ref, 0)
    _rope_heads(p_ref, hd, n_g, tab_ref, rows_ref, 0)
    rows_ref[:, gd:2 * gd] = p_ref[:, hd + gd:hd + 2 * gd]


def swa_prep(proj, tab, n_h, n_g, q_dtype):
    m, n_in = proj.shape
    tq = min(m, 256)
    hd, gd = n_h * LANES, n_g * LANES
    row = lambda w: pl.BlockSpec((tq, w), lambda i: (i, 0))
    return pl.pallas_call(
        functools.partial(_swa_prep_kernel, n_h=n_h, n_g=n_g),
        grid=(m // tq,),
        in_specs=[row(n_in), row(5 * LANES)],
        out_specs=[row(hd), row(2 * gd)],
        out_shape=[jax.ShapeDtypeStruct((m, hd), q_dtype), jax.ShapeDtypeStruct((m, 2 * gd), F32)],
        compiler_params=_params(1),
        name="swa_prep",
    )(proj, tab)


def _dsa_prep_kernel(p_ref, tab_ref, qr_ref, rows_ref, qi_ref, ki_ref, ki2_ref, wt_ref, *, n_h, n_g, n_ih, idx_dim):
    hd, gd = n_h * LANES, n_g * LANES
    half = idx_dim // 2
    _rope_heads(p_ref, 0, n_h, tab_ref, qr_ref, 0)
    _rope_heads(p_ref, hd, n_g, tab_ref, rows_ref, 0)
    rows_ref[:, gd:2 * gd] = p_ref[:, hd + gd:hd + 2 * gd]
    o3 = hd + 2 * gd
    for c in range(n_ih * idx_dim // LANES):
        x = p_ref[:, o3 + c * LANES:o3 + (c + 1) * LANES]
        qi_ref[:, c * LANES:(c + 1) * LANES] = _rope_idx(x, tab_ref, half).astype(qi_ref.dtype)
    o4 = o3 + n_ih * idx_dim
    tail = p_ref[:, o4:o4 + LANES]
    roped = _rope_idx(tail, tab_ref, half)
    ki_ref[...] = roped[:, 0:idx_dim]
    key_only = jnp.where(_iota((1, LANES), 1) < idx_dim, roped, 0.0)
    rep = key_only
    for u in range(1, LANES // idx_dim):
        rep = rep + pltpu.roll(key_only, u * idx_dim, 1)
    ki2_ref[...] = rep.astype(ki2_ref.dtype)
    wt_ref[...] = tail[:, idx_dim:idx_dim + n_ih] * (n_ih ** -0.5)


def dsa_prep(proj, tab, n_h, n_g, n_ih, idx_dim, q_dtype):
    m, n_in = proj.shape
    tq = min(m, 256)
    hd, gd = n_h * LANES, n_g * LANES
    assert (n_ih * idx_dim) % LANES == 0 and idx_dim + n_ih <= LANES and n_in % LANES == 0
    row = lambda w: pl.BlockSpec((tq, w), lambda i: (i, 0))
    return pl.pallas_call(
        functools.partial(_dsa_prep_kernel, n_h=n_h, n_g=n_g, n_ih=n_ih, idx_dim=idx_dim),
        grid=(m // tq,),
        in_specs=[row(n_in), row(5 * LANES)],
        out_specs=[row(hd), row(2 * gd), row(n_ih * idx_dim), row(idx_dim), row(LANES), row(n_ih)],
        out_shape=[jax.ShapeDtypeStruct((m, hd), q_dtype), jax.ShapeDtypeStruct((m, 2 * gd), F32),
                   jax.ShapeDtypeStruct((m, n_ih * idx_dim), q_dtype),
                   jax.ShapeDtypeStruct((m, idx_dim), F32), jax.ShapeDtypeStruct((m, LANES), BF16),
                   jax.ShapeDtypeStruct((m, n_ih), F32)],
        compiler_params=_params(1),
        name="dsa_prep",
    )(proj, tab)


def _softmax_parts(s, valid, sink=None):
    sm = jnp.where(valid, s, MASK_NEG)
    m = jnp.max(sm, axis=-1, keepdims=True)
    if sink is not None:
        m = jnp.maximum(m, sink)
    m = jnp.where(m < 0.5 * MASK_NEG, 0.0, m)
    e = jnp.exp(sm - m)
    den = jnp.sum(e, axis=-1, keepdims=True)
    if sink is not None:
        den = den + jnp.exp(sink - m)
    return e, jnp.maximum(den, 1e-30)


def _count(mask):
    return jnp.sum(jnp.where(mask, 1.0, 0.0), axis=-1, keepdims=True)


def _topk_mask(score, k):
    w = score.shape[-1]
    bits = lax.bitcast_convert_type(score, I32)
    key = jnp.where(bits < 0, bits ^ 0x7FFFFFFF, bits)
    kf = float(k)
    t0 = jnp.where(_count(key >= 0) >= kf, 0, INT_MIN).astype(I32)

    def value_bit(i, t):
        cand = t | jnp.left_shift(jnp.int32(1), 30 - i)
        return jnp.where(_count(key >= cand) >= kf, cand, t)

    t = lax.fori_loop(0, 31, value_bit, t0)
    above = key > t
    tie = key == t
    need = kf - _count(above)
    idx = _iota(score.shape, score.ndim - 1)
    nb = w.bit_length()

    def index_bit(i, lim):
        cand = lim | jnp.left_shift(jnp.int32(1), nb - 1 - i)
        return jnp.where(_count(tie & (idx < cand)) <= need, cand, lim)

    lim = lax.fori_loop(0, nb, index_bit, jnp.zeros_like(t0))
    return above | (tie & (idx < lim))


def _head_col(x, h):
    return jnp.sum(jnp.where(_iota(x.shape, 1) == h, x, 0.0), axis=1, keepdims=True)


def _pool_rows(x, alpha, cmp_blk):
    n = x.shape[0] // cmp_blk
    return jnp.sum(x.reshape(n, cmp_blk, x.shape[1]) * alpha[None], axis=1)


def _pool_prompt_kernel(rows_ref, alpha_ref, o_ref, *, n_cmp, cmp_blk):
    pooled = _pool_rows(rows_ref[0:n_cmp * cmp_blk, :], alpha_ref[...], cmp_blk)
    o_ref[0:n_cmp, :] = pooled
    if o_ref.shape[0] > n_cmp:
        o_ref[n_cmp:, :] = jnp.zeros((o_ref.shape[0] - n_cmp, o_ref.shape[1]), F32)


def pool_prompt(rows, alpha, n_cmp, n_cmp_pad, cmp_blk):
    b, t, wide = rows.shape
    half = wide // 2
    return pl.pallas_call(
        functools.partial(_pool_prompt_kernel, n_cmp=n_cmp, cmp_blk=cmp_blk),
        grid=(b,),
        in_specs=[pl.BlockSpec((None, t, half), lambda i: (i, 0, 0)),
                  pl.BlockSpec((cmp_blk, half), lambda i: (0, 0))],
        out_specs=pl.BlockSpec((None, n_cmp_pad, half), lambda i: (i, 0, 0)),
        out_shape=jax.ShapeDtypeStruct((b, n_cmp_pad, half), F32),
        compiler_params=_params(1),
        name="nsa_pool_prompt",
    )(rows, alpha)


def _pool_pages_kernel(pt_ref, *refs, n_pp, cmp_blk):
    del pt_ref
    pages, alpha_ref, o_ref = refs[:n_pp], refs[n_pp], refs[n_pp + 1]
    per_page = pages[0].shape[0] // cmp_blk
    for p in range(n_pp):
        o_ref[p * per_page:(p + 1) * per_page, :] = _pool_rows(pages[p][...], alpha_ref[...], cmp_blk)


def _page_specs(n_pp, layer, block, col):
    def spec(j):
        return pl.BlockSpec((None, None) + block,
                            lambda b, s, pt: (layer, pt[b, s * n_pp + j], 0, col))
    return [spec(j) for j in range(n_pp)]


def pool_pages(cache, layer, page_table, alpha, cmp_blk):
    _, _, page, wide = cache.shape
    b, n_pages = page_table.shape
    half = wide // 2
    n_pp = PAGES_PER_STEP
    assert n_pages % n_pp == 0 and page % cmp_blk == 0
    per_step = n_pp * page // cmp_blk
    return pl.pallas_call(
        functools.partial(_pool_pages_kernel, n_pp=n_pp, cmp_blk=cmp_blk),
        grid_spec=pltpu.PrefetchScalarGridSpec(
            num_scalar_prefetch=1,
            grid=(b, n_pages // n_pp),
            in_specs=_page_specs(n_pp, layer, (page, half), 0)
            + [pl.BlockSpec((cmp_blk, half), lambda bi, s, pt: (0, 0))],
            out_specs=pl.BlockSpec((None, per_step, half), lambda bi, s, pt: (bi, s, 0)),
        ),
        out_shape=jax.ShapeDtypeStruct((b, n_pages * page // cmp_blk, half), F32),
        compiler_params=_params(2),
        name="nsa_pool_pages",
    )(page_table, *([cache] * n_pp), alpha)


def _nsa_member(imp, q_pos, n_slc, n_sel, width):
    pair = imp + pltpu.roll(imp, imp.shape[1] - 1, 1)
    if width > imp.shape[1]:
        pair = jnp.concatenate([pair, jnp.zeros((imp.shape[0], width - imp.shape[1]), F32)], axis=1)
    c = _iota((1, width), 1)
    j = c >> 1
    sel_shift = _log2(NSA_SEL_BLOCK)
    visible = ((c & 1) == 0) & (j < n_slc) & ((j << sel_shift) <= q_pos)
    cur = q_pos >> sel_shift
    forced = (j == 0) | (j == cur) | (j == cur - 1)
    score = jnp.where(visible, pair + jnp.where(forced, NSA_FORCED_BONUS, 0.0), -jnp.inf)
    return _topk_mask(score, n_sel) & visible


def _expand_blocks(member, key0, n_keys):
    w = member.shape[1]
    c = _iota((w, n_keys), 0)
    k = key0 + _iota((w, n_keys), 1)
    e = (((c & 1) == 0) & ((c >> 1) == (k >> _log2(NSA_SEL_BLOCK)))).astype(BF16)
    return _dot(member.astype(BF16), e) > 0.5


def _nsa_prompt_kernel(q_ref, qr_ref, gate_ref, pooled_ref, wk_ref, wv_ref, sel_ref, win_ref,
                       o_ref, *, t, tq, n_h, n_g, n_cmp, n_slc, n_sel, cmp_blk):
    dh = LANES
    r = n_h // n_g
    gd = n_g * dh
    scale = dh ** -0.5
    w = pooled_ref.shape[0]
    i = pl.program_id(1)
    q_pos = i * tq + _iota((tq, 1), 0)
    c = _iota((1, w), 1)
    cmp_valid = ((c + 1) * cmp_blk - 1 <= q_pos) & (c < n_cmp)
    causal = _iota((1, t), 1) <= q_pos
    band = min(t, NSA_WINDOW + tq)
    start = pl.multiple_of(jnp.maximum(i * tq + tq - band, 0), tq)
    rel = q_pos - (start + _iota((1, band), 1))
    win_valid = (rel >= 0) & (rel < NSA_WINDOW)
    gates = gate_ref[...]
    wk = wk_ref[...].astype(BF16)
    wv = wv_ref[...].astype(BF16)

    for g in range(n_g):
        kc = _dot(pooled_ref[:, g * dh:(g + 1) * dh].astype(BF16), wk).astype(BF16)
        vc = _dot(pooled_ref[:, gd + g * dh:gd + (g + 1) * dh].astype(BF16), wv).astype(BF16)

        def cmp_head(rr, imp):
            h = g * r + rr
            col = pl.multiple_of(h * dh, dh)
            e, den = _softmax_parts(_dot_nt(q_ref[:, pl.ds(col, dh)], kc) * scale, cmp_valid)
            p = e / den
            o_c = _dot(p.astype(BF16), vc)
            o_ref[:, pl.ds(col, dh)] = (_head_col(gates, h) * o_c).astype(o_ref.dtype)
            return imp + p

        imp = lax.fori_loop(0, r, cmp_head, jnp.zeros((tq, w), F32))
        member = _nsa_member(imp, q_pos, n_slc, n_sel, w)
        sel_valid = _expand_blocks(member, 0, t) & causal
        k_sel = sel_ref[:, g * dh:(g + 1) * dh].astype(BF16)
        v_sel = sel_ref[:, gd + g * dh:gd + (g + 1) * dh].astype(BF16)
        k_win = win_ref[pl.ds(start, band), g * dh:(g + 1) * dh].astype(BF16)
        v_win = win_ref[pl.ds(start, band), gd + g * dh:gd + (g + 1) * dh].astype(BF16)

        def sparse_head(rr, carry):
            h = g * r + rr
            col = pl.multiple_of(h * dh, dh)
            qr = qr_ref[:, pl.ds(col, dh)]
            e, den = _softmax_parts(_dot_nt(qr, k_sel) * scale, sel_valid)
            o_s = _dot(e.astype(BF16), v_sel) / den
            e, den = _softmax_parts(_dot_nt(qr, k_win) * scale, win_valid)
            o_w = _dot(e.astype(BF16), v_win) / den
            acc = o_ref[:, pl.ds(col, dh)].astype(F32)
            acc = acc + _head_col(gates, n_h + h) * o_s + _head_col(gates, 2 * n_h + h) * o_w
            o_ref[:, pl.ds(col, dh)] = acc.astype(o_ref.dtype)
            return carry

        lax.fori_loop(0, r, sparse_head, 0)


def nsa_prompt_attention(q, qr, gates, pooled, wk, wv, rows, win, *, b, t, n_h, n_g, n_cmp, cmp_blk):
    dh = LANES
    hd, gd = n_h * dh, n_g * dh
    tq = min(Q_TILE, t)
    nq = t // tq
    n_slc = -(-t // NSA_SEL_BLOCK)
    n_sel = min(NSA_N_SEL, n_slc)
    assert n_cmp % 2 == 0 and 2 * n_slc <= pooled.shape[1] and NSA_SEL_BLOCK == 2 * cmp_blk
    kern = functools.partial(_nsa_prompt_kernel, t=t, tq=tq, n_h=n_h, n_g=n_g, n_cmp=n_cmp,
                             n_slc=n_slc, n_sel=n_sel, cmp_blk=cmp_blk)
    qspec = pl.BlockSpec((tq, hd), lambda bi, i: (bi * nq + i, 0))
    return pl.pallas_call(
        kern,
        grid=(b, nq),
        in_specs=[qspec, qspec,
                  pl.BlockSpec((tq, 3 * n_h), lambda bi, i: (bi * nq + i, 0)),
                  pl.BlockSpec((None,) + pooled.shape[1:], lambda bi, i: (bi, 0, 0)),
                  pl.BlockSpec((dh, dh), lambda bi, i: (0, 0)),
                  pl.BlockSpec((dh, dh), lambda bi, i: (0, 0)),
                  pl.BlockSpec((None, t, 2 * gd), lambda bi, i: (bi, 0, 1)),
                  pl.BlockSpec((None, t, 2 * gd), lambda bi, i: (bi, 0, 0))],
        out_specs=qspec,
        out_shape=jax.ShapeDtypeStruct((b * t, hd), F32),
        compiler_params=_params(2, 48),
        name="nsa_prompt_attention",
    )(q, qr, gates, pooled, wk, wv, rows, win)


def _swa_prompt_kernel(qr_ref, rows_ref, sink_ref, o_ref, *, t, tq, n_h, n_g):
    dh = LANES
    r = n_h // n_g
    gd = n_g * dh
    scale = dh ** -0.5
    i = pl.program_id(1)
    q_pos = i * tq + _iota((tq, 1), 0)
    band = min(t, SWA_WINDOW + tq)
    start = pl.multiple_of(jnp.maximum(i * tq + tq - band, 0), tq)
    rel = q_pos - (start + _iota((1, band), 1))
    valid = (rel >= 0) & (rel < SWA_WINDOW)
    sinks = sink_ref[...]
    for g in range(n_g):
        k = rows_ref[pl.ds(start, band), g * dh:(g + 1) * dh].astype(BF16)
        v = rows_ref[pl.ds(start, band), gd + g * dh:gd + (g + 1) * dh].astype(BF16)
        for rr in range(r):
            h = g * r + rr
            e, den = _softmax_parts(_dot_nt(qr_ref[:, h * dh:(h + 1) * dh], k) * scale, valid,
                                    sinks[:, h:h + 1])
            o_ref[:, h * dh:(h + 1) * dh] = (_dot(e.astype(BF16), v) / den).astype(o_ref.dtype)


def swa_prompt_attention(qr, rows, sinks, *, b, t, n_h, n_g):
    dh = LANES
    hd, gd = n_h * dh, n_g * dh
    tq = min(Q_TILE, t)
    nq = t // tq
    qspec = pl.BlockSpec((tq, hd), lambda bi, i: (bi * nq + i, 0))
    return pl.pallas_call(
        functools.partial(_swa_prompt_kernel, t=t, tq=tq, n_h=n_h, n_g=n_g),
        grid=(b, nq),
        in_specs=[qspec,
                  pl.BlockSpec((None, t, 2 * gd), lambda bi, i: (bi, 0, 0)),
                  pl.BlockSpec((1, n_h), lambda bi, i: (0, 0))],
        out_specs=qspec,
        out_shape=jax.ShapeDtypeStruct((b * t, hd), BF16),
        compiler_params=_params(2, 48),
        name="swa_prompt_attention",
    )(qr, rows, sinks.reshape(1, n_h))


def _indexer_scores(qi, ki2, wt, n_ih, idx_dim):
    per = LANES // idx_dim
    lane = _iota((1, LANES), 1)
    acc = None
    for c in range(n_ih // per):
        pair = qi[:, c * LANES:(c + 1) * LANES]
        for u in range(per):
            h = c * per + u
            qh = jnp.where((lane >= u * idx_dim) & (lane < (u + 1) * idx_dim), pair, jnp.zeros_like(pair))
            sc = _dot_nt(qh, ki2) * (idx_dim ** -0.5)
            term = jnp.maximum(sc, 0.0) * wt[:, h:h + 1]
            acc = term if acc is None else acc + term
    return acc


def _dsa_prompt_kernel(qr_ref, qi_ref, wt_ref, ki_ref, rows_ref, o_ref, *, t, tq, n_h, n_g, n_ih,
                       idx_dim, n_keep):
    dh = LANES
    r = n_h // n_g
    gd = n_g * dh
    scale = dh ** -0.5
    i = pl.program_id(1)
    q_pos = i * tq + _iota((tq, 1), 0)
    causal = _iota((1, t), 1) <= q_pos
    score = _indexer_scores(qi_ref[...], ki_ref[...], wt_ref[...], n_ih, idx_dim)
    score = jnp.where(causal, score, -jnp.inf)
    valid = _topk_mask(score, n_keep) & causal
    for g in range(n_g):
        k = rows_ref[:, g * dh:(g + 1) * dh].astype(BF16)
        v = rows_ref[:, gd + g * dh:gd + (g + 1) * dh].astype(BF16)
        for rr in range(r):
            h = g * r + rr
            e, den = _softmax_parts(_dot_nt(qr_ref[:, h * dh:(h + 1) * dh], k) * scale, valid)
            o_ref[:, h * dh:(h + 1) * dh] = (_dot(e.astype(BF16), v) / den).astype(o_ref.dtype)


def dsa_prompt_attention(qr, qi, wt, ki2, rows, *, b, t, n_h, n_g, n_ih, idx_dim):
    dh = LANES
    hd, gd = n_h * dh, n_g * dh
    tq = min(Q_TILE, t)
    nq = t // tq
    n_keep = min(DSA_TOPK, t // 4)
    qrow = lambda wdt: pl.BlockSpec((tq, wdt), lambda bi, i: (bi * nq + i, 0))
    return pl.pallas_call(
        functools.partial(_dsa_prompt_kernel, t=t, tq=tq, n_h=n_h, n_g=n_g, n_ih=n_ih,
                          idx_dim=idx_dim, n_keep=n_keep),
        grid=(b, nq),
        in_specs=[qrow(hd), qrow(n_ih * idx_dim), qrow(n_ih),
                  pl.BlockSpec((None, t, LANES), lambda bi, i: (bi, 0, 0)),
                  pl.BlockSpec((None, t, 2 * gd), lambda bi, i: (bi, 0, 0))],
        out_specs=qrow(hd),
        out_shape=jax.ShapeDtypeStruct((b * t, hd), BF16),
        compiler_params=_params(2, 48),
        name="dsa_prompt_attention",
    )(qr, qi, wt, ki2, rows)


def _stack_heads(x_ref, g, r):
    return jnp.concatenate([x_ref[:, (g * r + rr) * LANES:(g * r + rr + 1) * LANES] for rr in range(r)], axis=0)


def _tile_rows(mask, r):
    return jnp.concatenate([jnp.where(mask, 1.0, 0.0)] * r, axis=0) > 0.5


def _online_update(m_ref, l_ref, acc_ref, g, s, valid, v):
    sm = jnp.where(valid, s, MASK_NEG)
    m_old = m_ref[g]
    m_new = jnp.maximum(m_old, jnp.max(sm, axis=-1, keepdims=True))
    alpha = jnp.exp(m_old - m_new)
    e = jnp.where(valid, jnp.exp(sm - m_new), 0.0)
    l_ref[g] = alpha * l_ref[g] + jnp.sum(e, axis=-1, keepdims=True)
    acc_ref[g] = alpha * acc_ref[g] + _dot(e.astype(BF16), v)
    m_ref[g] = m_new


def _online_init(m_ref, l_ref, acc_ref):
    m_ref[...] = jnp.full(m_ref.shape, MASK_NEG, F32)
    l_ref[...] = jnp.zeros(l_ref.shape, F32)
    acc_ref[...] = jnp.zeros(acc_ref.shape, F32)


def _cat_pages(pages, lo, hi):
    return jnp.concatenate([p[:, lo:hi] for p in pages], axis=0).astype(BF16)


def _nsa_sample_kernel(pt_ref, *refs, n_pp, n_steps, td, n_h, n_g, n_cmp, n_slc, n_sel, cmp_blk, past_len,
                       width):
    del pt_ref
    pages = refs[:n_pp]
    (q_ref, qr_ref, gate_ref, pooled_ref, wk_ref, wv_ref, new_ref, buf_ref, nwin_ref,
     o_ref, member_ref, oc_ref, m_ref, l_ref, acc_ref) = refs[n_pp:]
    dh = LANES
    r = n_h // n_g
    gd = n_g * dh
    scale = dh ** -0.5
    page = pages[0].shape[0]
    s_id = pl.program_id(1)
    tok = _iota((td, 1), 0)
    q_pos = past_len + tok

    @pl.when(s_id == 0)
    def _():
        _online_init(m_ref, l_ref, acc_ref)
        w = pooled_ref.shape[0]
        c = _iota((1, w), 1)
        cmp_valid = _tile_rows(((c + 1) * cmp_blk - 1 <= q_pos) & (c < n_cmp), r)
        wk = wk_ref[...].astype(BF16)
        wv = wv_ref[...].astype(BF16)
        for g in range(n_g):
            kc = _dot(pooled_ref[:, g * dh:(g + 1) * dh].astype(BF16), wk).astype(BF16)
            vc = _dot(pooled_ref[:, gd + g * dh:gd + (g + 1) * dh].astype(BF16), wv).astype(BF16)
            qg = _stack_heads(q_ref, g, r).astype(BF16)
            e, den = _softmax_parts(_dot_nt(qg, kc) * scale, cmp_valid)
            p = e / den
            oc_ref[g] = _dot(p.astype(BF16), vc)
            imp = jnp.sum(p.reshape(r, td, w), axis=0)
            member = _nsa_member(imp, q_pos, n_slc, n_sel, width)
            member_ref[g] = jnp.where(member, 1.0, 0.0)

    key0 = s_id * (n_pp * page)
    for g in range(n_g):
        qg = _stack_heads(qr_ref, g, r).astype(BF16)
        valid = _tile_rows(_expand_blocks(member_ref[g] > 0.5, key0, n_pp * page), r)
        k = _cat_pages(pages, g * dh, (g + 1) * dh)
        v = _cat_pages(pages, gd + g * dh, gd + (g + 1) * dh)
        _online_update(m_ref, l_ref, acc_ref, g, _dot_nt(qg, k) * scale, valid, v)

    @pl.when(s_id == n_steps - 1)
    def _():
        gates = gate_ref[...]
        new_pos = past_len + _iota((1, td), 1)
        w_len = buf_ref.shape[0]
        buf_pos = past_len - w_len + _iota((1, w_len), 1)
        rel = jnp.concatenate([q_pos - buf_pos, q_pos - new_pos], axis=1)
        win_valid = _tile_rows((rel >= 0) & (rel < NSA_WINDOW), r)
        for g in range(n_g):
            qg = _stack_heads(qr_ref, g, r).astype(BF16)
            new_sel = member_ref[g][:, 2 * (n_slc - 1):2 * (n_slc - 1) + 1] > 0.5
            valid = _tile_rows(new_sel & (new_pos <= q_pos), r)
            k = new_ref[:, (2 * n_g + g) * dh:(2 * n_g + g + 1) * dh].astype(BF16)
            v = new_ref[:, (3 * n_g + g) * dh:(3 * n_g + g + 1) * dh].astype(BF16)
            _online_update(m_ref, l_ref, acc_ref, g, _dot_nt(qg, k) * scale, valid, v)
            o_s = acc_ref[g] / jnp.maximum(l_ref[g], 1e-30)
            k = jnp.concatenate([buf_ref[:, g * dh:(g + 1) * dh], nwin_ref[:, g * dh:(g + 1) * dh]],
                                axis=0).astype(BF16)
            v = jnp.concatenate([buf_ref[:, gd + g * dh:gd + (g + 1) * dh],
                                 nwin_ref[:, gd + g * dh:gd + (g + 1) * dh]], axis=0).astype(BF16)
            e, den = _softmax_parts(_dot_nt(qg, k) * scale, win_valid)
            o_w = _dot(e.astype(BF16), v) / den
            o_c = oc_ref[g]
            for rr in range(r):
                h = g * r + rr
                rows = slice(rr * td, (rr + 1) * td)
                o = (gates[:, h:h + 1] * o_c[rows] + gates[:, n_h + h:n_h + h + 1] * o_s[rows]
                     + gates[:, 2 * n_h + h:2 * n_h + h + 1] * o_w[rows])
                o_ref[:, h * dh:(h + 1) * dh] = o.astype(o_ref.dtype)


def nsa_sample_attention(q, qr, gates, pooled, wk, wv, new_rows, win_buf, new_win, cache, layer, page_table,
                         *, n_h, n_g, cmp_blk):
    dh = LANES
    b, td, hd = q.shape
    gd = n_g * dh
    r = n_h // n_g
    _, n_pages = page_table.shape
    page = cache.shape[2]
    past_len = n_pages * page
    seq_len = past_len + td
    n_cmp = seq_len // cmp_blk
    n_slc = -(-seq_len // NSA_SEL_BLOCK)
    n_sel = min(NSA_N_SEL, n_slc)
    n_pp = PAGES_PER_STEP
    n_steps = n_pages // n_pp
    assert past_len % NSA_SEL_BLOCK == 0 and td <= cmp_blk and n_cmp == pooled.shape[1]
    assert n_cmp % LANES == 0 and n_pages % n_pp == 0 and NSA_SEL_BLOCK == 2 * cmp_blk
    width = n_cmp + LANES
    full = lambda shape: pl.BlockSpec((None,) + shape, lambda bi, s, pt: (bi, 0, 0))
    const = lambda shape: pl.BlockSpec(shape, lambda bi, s, pt: (0, 0))
    kern = functools.partial(_nsa_sample_kernel, n_pp=n_pp, n_steps=n_steps, td=td, n_h=n_h, n_g=n_g,
                             n_cmp=n_cmp, n_slc=n_slc, n_sel=n_sel, cmp_blk=cmp_blk, past_len=past_len,
                             width=width)
    return pl.pallas_call(
        kern,
        grid_spec=pltpu.PrefetchScalarGridSpec(
            num_scalar_prefetch=1,
            grid=(b, n_steps),
            in_specs=_page_specs(n_pp, layer, (page, 2 * gd), 1)
            + [full((td, hd)), full((td, hd)), full((td, 3 * n_h)), full((n_cmp, 2 * gd)),
               const((dh, dh)), const((dh, dh)), full((td, 4 * gd)), full(win_buf.shape[1:]),
               full((td, 2 * gd))],
            out_specs=full((td, hd)),
            scratch_shapes=[pltpu.VMEM((n_g, td, width), F32), pltpu.VMEM((n_g, r * td, dh), F32),
                            pltpu.VMEM((n_g, r * td, 1), F32), pltpu.VMEM((n_g, r * td, 1), F32),
                            pltpu.VMEM((n_g, r * td, dh), F32)],
        ),
        out_shape=jax.ShapeDtypeStruct((b, td, hd), BF16),
        compiler_params=_params(2, 48),
        name="nsa_sample_attention",
    )(page_table, *([cache] * n_pp), q, qr, gates, pooled, wk, wv, new_rows, win_buf, new_win)


def _swa_sample_kernel(qr_ref, buf_ref, new_ref, sink_ref, o_ref, *, td, n_h, n_g, past_len):
    dh = LANES
    r = n_h // n_g
    gd = n_g * dh
    scale = dh ** -0.5
    q_pos = past_len + _iota((td, 1), 0)
    w_len = buf_ref.shape[0]
    k_pos = jnp.concatenate([past_len - w_len + _iota((1, w_len), 1), past_len + _iota((1, td), 1)], axis=1)
    rel = q_pos - k_pos
    valid = _tile_rows((rel >= 0) & (rel < SWA_WINDOW) & (k_pos >= 0), r)
    sinks = sink_ref[...]
    for g in range(n_g):
        qg = _stack_heads(qr_ref, g, r).astype(BF16)
        k = jnp.concatenate([buf_ref[:, g * dh:(g + 1) * dh], new_ref[:, g * dh:(g + 1) * dh]],
                            axis=0).astype(BF16)
        v = jnp.concatenate([buf_ref[:, gd + g * dh:gd + (g + 1) * dh],
                             new_ref[:, gd + g * dh:gd + (g + 1) * dh]], axis=0).astype(BF16)
        sink = jnp.concatenate([jnp.broadcast_to(sinks[:, g * r + rr:g * r + rr + 1], (td, 1))
                                for rr in range(r)], axis=0)
        e, den = _softmax_parts(_dot_nt(qg, k) * scale, valid, sink)
        o = _dot(e.astype(BF16), v) / den
        for rr in range(r):
            h = g * r + rr
            o_ref[:, h * dh:(h + 1) * dh] = o[rr * td:(rr + 1) * td].astype(o_ref.dtype)


def swa_sample_attention(qr, buf, new_rows, sinks, *, n_h, n_g, past_len):
    b, td, hd = qr.shape
    full = lambda shape: pl.BlockSpec((None,) + shape, lambda bi: (bi, 0, 0))
    return pl.pallas_call(
        functools.partial(_swa_sample_kernel, td=td, n_h=n_h, n_g=n_g, past_len=past_len),
        grid=(b,),
        in_specs=[full((td, hd)), full(buf.shape[1:]), full(new_rows.shape[1:]),
                  pl.BlockSpec((1, n_h), lambda bi: (0, 0))],
        out_specs=full((td, hd)),
        out_shape=jax.ShapeDtypeStruct((b, td, hd), BF16),
        compiler_params=_params(1),
        name="swa_sample_attention",
    )(qr, buf, new_rows, sinks.reshape(1, n_h))


def _dsa_select_kernel(pt_ref, *refs, n_pp, n_steps, td, n_ih, idx_dim, n_keep):
    del pt_ref
    pages = refs[:n_pp]
    qi_ref, wt_ref, kin_ref, o_ref, score_ref = refs[n_pp:]
    page = pages[0].shape[0]
    s_id = pl.program_id(1)
    chunk = n_pp * page
    n_past = n_steps * chunk

    def scores(ki):
        sc = _dot_nt(qi_ref[...], ki.astype(BF16)) * (idx_dim ** -0.5)
        term = jnp.maximum(sc, 0.0) * wt_ref[...]
        return jnp.sum(term.reshape(n_ih, td, ki.shape[0]), axis=0)

    ki = jnp.concatenate([p[...] for p in pages], axis=0)
    score_ref[:, pl.ds(pl.multiple_of(s_id * chunk, LANES), chunk)] = scores(ki)

    @pl.when(s_id == n_steps - 1)
    def _():
        tok = _iota((td, 1), 0)
        lane = _iota((1, LANES), 1)
        score_ref[:, n_past:] = jnp.where((lane <= tok) & (lane < td), scores(kin_ref[...]), -jnp.inf)
        score = score_ref[...]
        member = _topk_mask(score, n_keep) & (score > -jnp.inf)
        o_ref[...] = jnp.where(member, 1.0, 0.0).astype(o_ref.dtype)


def dsa_sample_select(qi, wt, ki_new, cache_idx, layer, page_table, *, n_ih, idx_dim):
    b, n_pages = page_table.shape
    page = cache_idx.shape[2]
    td = qi.shape[1] // n_ih
    n_pp = PAGES_PER_STEP
    n_steps = n_pages // n_pp
    n_keys = n_pages * page + LANES
    n_keep = min(DSA_TOPK, (n_pages * page + td) // 4)
    full = lambda shape: pl.BlockSpec((None,) + shape, lambda bi, s, pt: (bi, 0, 0))
    return pl.pallas_call(
        functools.partial(_dsa_select_kernel, n_pp=n_pp, n_steps=n_steps, td=td, n_ih=n_ih, idx_dim=idx_dim,
                          n_keep=n_keep),
        grid_spec=pltpu.PrefetchScalarGridSpec(
            num_scalar_prefetch=1,
            grid=(b, n_steps),
            in_specs=_page_specs(n_pp, layer, (page, idx_dim), 0)
            + [full(qi.shape[1:]), full(wt.shape[1:]), full(ki_new.shape[1:])],
            out_specs=full((td, n_keys)),
            scratch_shapes=[pltpu.VMEM((td, n_keys), F32)],
        ),
        out_shape=jax.ShapeDtypeStruct((b, td, n_keys), BF16),
        compiler_params=_params(2),
        name="dsa_sample_select",
    )(page_table, *([cache_idx] * n_pp), qi, wt, ki_new)


def _dsa_sample_kernel(pt_ref, *refs, n_pp, n_steps, td, n_h, n_g):
    del pt_ref
    pages = refs[:n_pp]
    qr_ref, mem_ref, tail_ref, new_ref, o_ref, m_ref, l_ref, acc_ref = refs[n_pp:]
    dh = LANES
    r = n_h // n_g
    gd = n_g * dh
    scale = dh ** -0.5
    s_id = pl.program_id(1)

    @pl.when(s_id == 0)
    def _():
        _online_init(m_ref, l_ref, acc_ref)

    valid = _tile_rows(mem_ref[...].astype(F32) > 0.5, r)
    for g in range(n_g):
        qg = _stack_heads(qr_ref, g, r).astype(BF16)
        k = _cat_pages(pages, g * dh, (g + 1) * dh)
        v = _cat_pages(pages, gd + g * dh, gd + (g + 1) * dh)
        _online_update(m_ref, l_ref, acc_ref, g, _dot_nt(qg, k) * scale, valid, v)

    @pl.when(s_id == n_steps - 1)
    def _():
        new_valid = _tile_rows(tail_ref[...].astype(F32) > 0.5, r)
        for g in range(n_g):
            qg = _stack_heads(qr_ref, g, r).astype(BF16)
            k = new_ref[:, g * dh:(g + 1) * dh].astype(BF16)
            v = new_ref[:, gd + g * dh:gd + (g + 1) * dh].astype(BF16)
            _online_update(m_ref, l_ref, acc_ref, g, _dot_nt(qg, k) * scale, new_valid, v)
            o = acc_ref[g] / jnp.maximum(l_ref[g], 1e-30)
            for rr in range(r):
                h = g * r + rr
                o_ref[:, h * dh:(h + 1) * dh] = o[rr * td:(rr + 1) * td].astype(o_ref.dtype)


def dsa_sample_attention(qr, member, new_rows, cache, layer, page_table, *, n_h, n_g):
    dh = LANES
    b, td, hd = qr.shape
    gd = n_g * dh
    r = n_h // n_g
    _, n_pages = page_table.shape
    page = cache.shape[2]
    n_pp = PAGES_PER_STEP
    n_steps = n_pages // n_pp
    chunk = n_pp * page
    full = lambda shape: pl.BlockSpec((None,) + shape, lambda bi, s, pt: (bi, 0, 0))
    return pl.pallas_call(
        functools.partial(_dsa_sample_kernel, n_pp=n_pp, n_steps=n_steps, td=td, n_h=n_h, n_g=n_g),
        grid_spec=pltpu.PrefetchScalarGridSpec(
            num_scalar_prefetch=1,
            grid=(b, n_steps),
            in_specs=_page_specs(n_pp, layer, (page, 2 * gd), 0)
            + [full((td, hd)),
               pl.BlockSpec((None, td, chunk), lambda bi, s, pt: (bi, 0, s)),
               pl.BlockSpec((None, td, LANES), lambda bi, s, pt: (bi, 0, n_steps * chunk // LANES)),
               full(new_rows.shape[1:])],
            out_specs=full((td, hd)),
            scratch_shapes=[pltpu.VMEM((n_g, r * td, 1), F32), pltpu.VMEM((n_g, r * td, 1), F32),
                            pltpu.VMEM((n_g, r * td, dh), F32)],
        ),
        out_shape=jax.ShapeDtypeStruct((b, td, hd), BF16),
        compiler_params=_params(2, 48),
        name="dsa_sample_attention",
    )(page_table, *([cache] * n_pp), qr, member, member, new_rows)


def _pad_rows(x, n):
    return jnp.pad(x, ((0, 0), (0, n - x.shape[1]), (0, 0)))


def _run(x, pos, sample, weights, dims):
    (norm_mixer, norm_mlp, norm_final, w_mlp_up, w_mlp_down, nsa_w_in, nsa_w_out, nsa_cmp_pos_k,
     nsa_cmp_pos_v, nsa_cmp_wk, nsa_cmp_wv, swa_w_in, swa_w_out, swa_sinks, dsa_w_in, dsa_w_out) = weights
    n_h, g_nsa, g_swa, g_dsa, n_ih, idx_dim, cmp_blk = dims
    b, t, d = x.shape
    m = b * t
    dh = LANES
    hd = n_h * dh
    depth = norm_mixer.shape[0]
    q_dtype = BF16 if sample is None else F32
    tab = jnp.tile(rope_tables(pos, dh, idx_dim), (b, 1))
    x = x.reshape(m, d)
    nsa_rows, nsa_win, swa_win, dsa_rows, dsa_idx = [], [], [], [], []
    for i in range(depth):
        kind, j = i % 3, i // 3
        h = rmsnorm(x, norm_mixer[i], BF16)
        if kind == 0:
            gd = g_nsa * dh
            proj = matmul(h, nsa_w_in[j])
            q, qr, rows, win, gates = nsa_prep(proj, tab, n_h, g_nsa, q_dtype)
            alpha = jnp.concatenate([jnp.broadcast_to(nsa_cmp_pos_k[j][:, None], (cmp_blk, gd)),
                                     jnp.broadcast_to(nsa_cmp_pos_v[j][:, None], (cmp_blk, gd))], axis=1)
            if sample is None:
                n_cmp = t // cmp_blk
                n_cmp_pad = -(-n_cmp // LANES) * LANES
                pooled = pool_prompt(rows.reshape(b, t, 4 * gd), alpha, n_cmp, n_cmp_pad, cmp_blk)
                o = nsa_prompt_attention(q, qr, gates, pooled, nsa_cmp_wk[j], nsa_cmp_wv[j],
                                         rows.reshape(b, t, 4 * gd), win.reshape(b, t, 2 * gd),
                                         b=b, t=t, n_h=n_h, n_g=g_nsa, n_cmp=n_cmp, cmp_blk=cmp_blk)
                nsa_win.append(win.reshape(b, t, 2, g_nsa, dh)[:, -min(NSA_WINDOW, t):])
            else:
                cache = sample["cache_nsa"]
                cache2 = cache.reshape(cache.shape[0], cache.shape[1], cache.shape[2], 4 * gd)
                buf = sample["state_nsa_win"][j]
                pooled = pool_pages(cache2, j, sample["page_table"], alpha, cmp_blk)
                o = nsa_sample_attention(q.reshape(b, t, hd), qr.reshape(b, t, hd), gates.reshape(b, t, 3 * n_h),
                                         pooled, nsa_cmp_wk[j], nsa_cmp_wv[j], rows.reshape(b, t, 4 * gd),
                                         buf.reshape(b, buf.shape[1], 2 * gd), win.reshape(b, t, 2 * gd),
                                         cache2, j, sample["page_table"], n_h=n_h, n_g=g_nsa, cmp_blk=cmp_blk)
                o = o.reshape(m, hd)
                nsa_win.append(jnp.concatenate([buf, win.reshape(b, t, 2, g_nsa, dh)], axis=1)[:, -buf.shape[1]:])
            nsa_rows.append(rows.reshape(b, t, 4, g_nsa, dh))
            w_out = nsa_w_out[j]
        elif kind == 1:
            gd = g_swa * dh
            proj = matmul(h, swa_w_in[j])
            qr, rows = swa_prep(proj, tab, n_h, g_swa, q_dtype)
            if sample is None:
                o = swa_prompt_attention(qr, rows.reshape(b, t, 2 * gd), swa_sinks[j], b=b, t=t, n_h=n_h,
                                         n_g=g_swa)
                swa_win.append(rows.reshape(b, t, 2, g_swa, dh)[:, -min(SWA_WINDOW, t):])
            else:
                buf = sample["state_swa"][j]
                o = swa_sample_attention(qr.reshape(b, t, hd), buf.reshape(b, buf.shape[1], 2 * gd),
                                         rows.reshape(b, t, 2 * gd), swa_sinks[j], n_h=n_h, n_g=g_swa,
                                         past_len=sample["past_len"]).reshape(m, hd)
                swa_win.append(jnp.concatenate([buf, rows.reshape(b, t, 2, g_swa, dh)], axis=1)[:, -buf.shape[1]:])
            w_out = swa_w_out[j]
        else:
            gd = g_dsa * dh
            n_in = dsa_w_in.shape[-1]
            w_in = jnp.pad(dsa_w_in[j], ((0, 0), (0, -n_in % LANES)))
            proj = matmul(h, w_in)
            qr, rows, qi, ki, ki2, wt = dsa_prep(proj, tab, n_h, g_dsa, n_ih, idx_dim, q_dtype)
            if sample is None:
                o = dsa_prompt_attention(qr, qi, wt, ki2.reshape(b, t, LANES), rows.reshape(b, t, 2 * gd),
                                         b=b, t=t, n_h=n_h, n_g=g_dsa, n_ih=n_ih, idx_dim=idx_dim)
            else:
                cache = sample["cache_dsa"]
                cache2 = cache.reshape(cache.shape[0], cache.shape[1], cache.shape[2], 2 * gd)
                qi_rows = qi.reshape(b, t, n_ih, idx_dim).transpose(0, 2, 1, 3).reshape(b, n_ih * t, idx_dim)
                wt_rows = wt.reshape(b, t, n_ih).transpose(0, 2, 1).reshape(b, n_ih * t, 1)
                member = dsa_sample_select(qi_rows.astype(BF16), wt_rows, _pad_rows(ki.reshape(b, t, idx_dim), LANES),
                                           sample["cache_dsa_idx"], j, sample["page_table"], n_ih=n_ih,
                                           idx_dim=idx_dim)
                o = dsa_sample_attention(qr.reshape(b, t, hd), member, _pad_rows(rows.reshape(b, t, 2 * gd), LANES),
                                         cache2, j, sample["page_table"], n_h=n_h, n_g=g_dsa).reshape(m, hd)
            dsa_rows.append(rows.reshape(b, t, 2, g_dsa, dh))
            dsa_idx.append(ki.reshape(b, t, idx_dim))
            w_out = dsa_w_out[j]
        x = matmul(o, w_out, residual=x)
        h = rmsnorm(x, norm_mlp[i], BF16)
        u = matmul(h, w_mlp_up[i], act="relu2", out_dtype=BF16)
        x = matmul(u, w_mlp_down[i], residual=x)
    y = rmsnorm(x, norm_final, F32).reshape(b, t, d)
    return y, (jnp.stack(nsa_rows), jnp.stack(nsa_win), jnp.stack(swa_win), jnp.stack(dsa_rows),
               jnp.stack(dsa_idx))


def kernel(x_prompt, x_sample, cache_nsa, state_nsa_win, state_swa, cache_dsa, cache_dsa_idx, page_table, norm_mixer, norm_mlp, norm_final, w_mlp_up, w_mlp_down, nsa_w_in, nsa_w_out, nsa_cmp_pos_k, nsa_cmp_pos_v, nsa_cmp_wk, nsa_cmp_wv, swa_w_in, swa_w_out, swa_sinks, dsa_w_in, dsa_w_out):
    n_h = swa_sinks.shape[-1]
    dh = cache_nsa.shape[-1]
    assert dh == LANES and x_prompt.shape[-1] == n_h * dh
    g_nsa, g_swa, g_dsa = cache_nsa.shape[4], state_swa.shape[4], cache_dsa.shape[4]
    idx_dim = cache_dsa_idx.shape[-1]
    n_ih = (dsa_w_in.shape[-1] - n_h * dh - 2 * g_dsa * dh - idx_dim) // (idx_dim + 1)
    cmp_blk = nsa_cmp_pos_k.shape[-1]
    dims = (n_h, g_nsa, g_swa, g_dsa, n_ih, idx_dim, cmp_blk)
    weights = (norm_mixer, norm_mlp, norm_final, w_mlp_up, w_mlp_down, nsa_w_in, nsa_w_out, nsa_cmp_pos_k,
               nsa_cmp_pos_v, nsa_cmp_wk, nsa_cmp_wv, swa_w_in, swa_w_out, swa_sinks, dsa_w_in, dsa_w_out)
    past_len = page_table.shape[1] * cache_nsa.shape[2]
    sample = dict(cache_nsa=cache_nsa, state_nsa_win=state_nsa_win, state_swa=state_swa, cache_dsa=cache_dsa,
                  cache_dsa_idx=cache_dsa_idx, page_table=page_table, past_len=past_len)
    y_p, (p_nsa, p_nsa_win, p_swa, p_dsa, p_dsa_idx) = _run(
        x_prompt, jnp.arange(x_prompt.shape[1], dtype=jnp.int32), None, weights, dims)
    y_s, (s_nsa, s_nsa_win, s_swa, s_dsa, s_dsa_idx) = _run(
        x_sample, past_len + jnp.arange(x_sample.shape[1], dtype=jnp.int32), sample, weights, dims)
    return (y_p, y_s, p_nsa, s_nsa, p_nsa_win, s_nsa_win, p_swa, s_swa, p_dsa, s_dsa, p_dsa_idx, s_dsa_idx)
```

```python
import functools

import jax
import jax.numpy as jnp
from jax import lax
from jax.experimental import pallas as pl
from jax.experimental.pallas import tpu as pltpu

F32 = jnp.float32
BF16 = jnp.bfloat16
I32 = jnp.int32

ROPE_THETA = 10000.0
NORM_EPS = 1e-6
NSA_SEL_BLOCK = 64
NSA_N_SEL = 16
NSA_WINDOW = 512
NSA_FORCED_BONUS = 1e4
SWA_WINDOW = 128
DSA_TOPK = 256
Q_TILE = 128
KEY_CHUNK = 512
NSA_PAGES_PER_STEP = 32
DSA_PAGES_PER_STEP = 16
IDX_PAGES_PER_STEP = 32

LANES = 128
MASK_NEG = -1e30
INT_MIN = -2 ** 31
MIB = 1024 * 1024


def _params(n_axes, vmem_mib=32):
    return pltpu.CompilerParams(dimension_semantics=("arbitrary",) * n_axes,
                                vmem_limit_bytes=vmem_mib * MIB)


def _dot(a, b):
    return jnp.dot(a, b, preferred_element_type=F32)


def _dot_nt(a, b):
    return lax.dot_general(a, b, (((1,), (1,)), ((), ())), preferred_element_type=F32)


def _iota(shape, dim):
    return lax.broadcasted_iota(I32, shape, dim)


def _log2(n):
    assert n > 0 and n & (n - 1) == 0, n
    return n.bit_length() - 1


def _rmsnorm_kernel(x_ref, g_ref, o_ref):
    x = x_ref[...]
    y = x * lax.rsqrt(jnp.mean(x * x, axis=-1, keepdims=True) + NORM_EPS)
    o_ref[...] = (y * g_ref[...]).astype(o_ref.dtype)


def rmsnorm(x, g, out_dtype):
    m, d = x.shape
    tm = min(m, 256)
    return pl.pallas_call(
        _rmsnorm_kernel,
        grid=(m // tm,),
        in_specs=[pl.BlockSpec((tm, d), lambda i: (i, 0)),
                  pl.BlockSpec((1, d), lambda i: (0, 0))],
        out_specs=pl.BlockSpec((tm, d), lambda i: (i, 0)),
        out_shape=jax.ShapeDtypeStruct((m, d), out_dtype),
        compiler_params=_params(1),
        name="rmsnorm",
    )(x, g.reshape(1, d))


def _mm_kernel(*refs, nk, act, has_res):
    a_ref, w_ref = refs[0], refs[1]
    res_ref = refs[2] if has_res else None
    o_ref = refs[2 + has_res]
    scratch_ref = refs[3 + has_res]
    if nk == 1:
        @pl.when(pl.program_id(1) == 0)
        def _():
            scratch_ref[...] = w_ref[...].astype(BF16)

        part = _dot(a_ref[...].astype(BF16), scratch_ref[...])
    else:
        acc_ref = scratch_ref
        part = _dot(a_ref[...].astype(BF16), w_ref[...].astype(BF16))

    def finish(y):
        if act == "relu2":
            y = jnp.maximum(y, 0.0)
            y = y * y
        if has_res:
            y = y + res_ref[...]
        o_ref[...] = y.astype(o_ref.dtype)

    if nk == 1:
        finish(part)
        return
    k = pl.program_id(2)

    @pl.when(k == 0)
    def _():
        acc_ref[...] = part

    @pl.when(k > 0)
    def _():
        acc_ref[...] += part

    @pl.when(k == nk - 1)
    def _():
        finish(acc_ref[...])


def matmul(a, w, *, residual=None, act=None, out_dtype=F32, tm=1024, tn=1024, tk=2048):
    m, kd = a.shape
    n = w.shape[1]
    tm, tn, tk = min(tm, m), min(tn, n), min(tk, kd)
    assert m % tm == 0 and kd % tk == 0
    nk = kd // tk
    in_specs = [pl.BlockSpec((tm, tk), lambda j, i, k: (i, k)),
                pl.BlockSpec((tk, tn), lambda j, i, k: (k, j))]
    args = [a, w]
    if residual is not None:
        in_specs.append(pl.BlockSpec((tm, tn), lambda j, i, k: (i, j)))
        args.append(residual)
    return pl.pallas_call(
        functools.partial(_mm_kernel, nk=nk, act=act, has_res=residual is not None),
        grid=(pl.cdiv(n, tn), m // tm, nk),
        in_specs=in_specs,
        out_specs=pl.BlockSpec((tm, tn), lambda j, i, k: (i, j)),
        out_shape=jax.ShapeDtypeStruct((m, n), out_dtype),
        scratch_shapes=[pltpu.VMEM((tm, tn), F32) if nk > 1 else pltpu.VMEM((tk, tn), BF16)],
        compiler_params=_params(3, 56),
        name="matmul",
    )(*args)


def rope_tables(pos, head_dim, idx_dim):
    def angles(d):
        inv = ROPE_THETA ** (-jnp.arange(0, d, 2, dtype=F32) / d)
        ang = pos.astype(F32)[:, None] * inv[None, :]
        return jnp.cos(ang), jnp.sin(ang)

    assert head_dim == LANES and LANES % idx_dim == 0
    cos, sin = angles(head_dim)
    rep = LANES // idx_dim
    cos_i, sin_i = angles(idx_dim)
    zero = jnp.zeros_like(sin_i)
    return jnp.concatenate([
        cos, cos, -sin, sin,
        jnp.tile(jnp.concatenate([cos_i, cos_i], 1), (1, rep)),
        jnp.tile(jnp.concatenate([zero, sin_i], 1), (1, rep)),
        jnp.tile(jnp.concatenate([-sin_i, zero], 1), (1, rep)),
    ], axis=1)


def _rope_heads(src_ref, col, n, tab_ref, dst_ref, dst_col):
    cos = tab_ref[:, 0:LANES]
    sin = tab_ref[:, LANES:2 * LANES]
    for h in range(n):
        x = src_ref[:, col + h * LANES:col + (h + 1) * LANES]
        y = x * cos + pltpu.roll(x, LANES // 2, 1) * sin
        dst_ref[:, dst_col + h * LANES:dst_col + (h + 1) * LANES] = y.astype(dst_ref.dtype)


def _rope_idx(x, tab_ref, half):
    cos = tab_ref[:, 2 * LANES:3 * LANES]
    sin_p = tab_ref[:, 3 * LANES:4 * LANES]
    sin_n = tab_ref[:, 4 * LANES:5 * LANES]
    return x * cos + pltpu.roll(x, half, 1) * sin_p + pltpu.roll(x, LANES - half, 1) * sin_n


def _nsa_prep_kernel(p_ref, tab_ref, q_ref, qr_ref, rows_ref, win_ref, gate_ref, *, n_h, n_g):
    hd = n_h * LANES
    gd = n_g * LANES
    q_ref[...] = p_ref[:, 0:hd].astype(q_ref.dtype)
    _rope_heads(p_ref, 0, n_h, tab_ref, qr_ref, 0)
    rows_ref[:, 0:2 * gd] = p_ref[:, hd:hd + 2 * gd]
    _rope_heads(p_ref, hd + 2 * gd, n_g, tab_ref, rows_ref, 2 * gd)
    rows_ref[:, 3 * gd:4 * gd] = p_ref[:, hd + 3 * gd:hd + 4 * gd]
    _rope_heads(p_ref, hd + 4 * gd, n_g, tab_ref, win_ref, 0)
    win_ref[:, gd:2 * gd] = p_ref[:, hd + 5 * gd:hd + 6 * gd]
    gate_ref[...] = jax.nn.sigmoid(p_ref[:, hd + 6 * gd:hd + 6 * gd + 3 * n_h])


def nsa_prep(proj, tab, n_h, n_g, q_dtype):
    m, n_in = proj.shape
    tq = min(m, 256)
    hd, gd = n_h * LANES, n_g * LANES
    row = lambda w: pl.BlockSpec((tq, w), lambda i: (i, 0))
    return pl.pallas_call(
        functools.partial(_nsa_prep_kernel, n_h=n_h, n_g=n_g),
        grid=(m // tq,),
        in_specs=[row(n_in), row(5 * LANES)],
        out_specs=[row(hd), row(hd), row(4 * gd), row(2 * gd), row(3 * n_h)],
        out_shape=[jax.ShapeDtypeStruct((m, hd), q_dtype), jax.ShapeDtypeStruct((m, hd), q_dtype),
                   jax.ShapeDtypeStruct((m, 4 * gd), F32), jax.ShapeDtypeStruct((m, 2 * gd), F32),
                   jax.ShapeDtypeStruct((m, 3 * n_h), F32)],
        compiler_params=_params(1),
        name="nsa_prep",
    )(proj, tab)


def _swa_prep_kernel(p_ref, tab_ref, qr_ref, rows_ref, *, n_h, n_g):
    hd, gd = n_h * LANES, n_g * LANES
    _rope_heads(p_ref, 0, n_h, tab_ref, qr_ref, 0)
    _rope_heads(p_ref, hd, n_g, tab_ref, rows_ref, 0)
    rows_ref[:, gd:2 * gd] = p_ref[:, hd + gd:hd + 2 * gd]


def swa_prep(proj, tab, n_h, n_g, q_dtype):
    m, n_in = proj.shape
    tq = min(m, 256)
    hd, gd = n_h * LANES, n_g * LANES
    row = lambda w: pl.BlockSpec((tq, w), lambda i: (i, 0))
    return pl.pallas_call(
        functools.partial(_swa_prep_kernel, n_h=n_h, n_g=n_g),
        grid=(m // tq,),
        in_specs=[row(n_in), row(5 * LANES)],
        out_specs=[row(hd), row(2 * gd)],
        out_shape=[jax.ShapeDtypeStruct((m, hd), q_dtype), jax.ShapeDtypeStruct((m, 2 * gd), F32)],
        compiler_params=_params(1),
        name="swa_prep",
    )(proj, tab)


def _dsa_prep_kernel(p_ref, tab_ref, qr_ref, rows_ref, qi_ref, ki_ref, ki2_ref, wt_ref, *, n_h, n_g, n_ih, idx_dim):
    hd, gd = n_h * LANES, n_g * LANES
    half = idx_dim // 2
    _rope_heads(p_ref, 0, n_h, tab_ref, qr_ref, 0)
    _rope_heads(p_ref, hd, n_g, tab_ref, rows_ref, 0)
    rows_ref[:, gd:2 * gd] = p_ref[:, hd + gd:hd + 2 * gd]
    o3 = hd + 2 * gd
    for c in range(n_ih * idx_dim // LANES):
        x = p_ref[:, o3 + c * LANES:o3 + (c + 1) * LANES]
        qi_ref[:, c * LANES:(c + 1) * LANES] = _rope_idx(x, tab_ref, half).astype(qi_ref.dtype)
    o4 = o3 + n_ih * idx_dim
    tail = p_ref[:, o4:o4 + LANES]
    roped = _rope_idx(tail, tab_ref, half)
    ki_ref[...] = roped[:, 0:idx_dim]
    key_only = jnp.where(_iota((1, LANES), 1) < idx_dim, roped, 0.0)
    rep = key_only
    for u in range(1, LANES // idx_dim):
        rep = rep + pltpu.roll(key_only, u * idx_dim, 1)
    ki2_ref[...] = rep.astype(ki2_ref.dtype)
    wt_ref[...] = tail[:, idx_dim:idx_dim + n_ih] * (n_ih ** -0.5)


def dsa_prep(proj, tab, n_h, n_g, n_ih, idx_dim, q_dtype):
    m, n_in = proj.shape
    tq = min(m, 256)
    hd, gd = n_h * LANES, n_g * LANES
    assert (n_ih * idx_dim) % LANES == 0 and idx_dim + n_ih <= LANES and n_in % LANES == 0
    row = lambda w: pl.BlockSpec((tq, w), lambda i: (i, 0))
    return pl.pallas_call(
        functools.partial(_dsa_prep_kernel, n_h=n_h, n_g=n_g, n_ih=n_ih, idx_dim=idx_dim),
        grid=(m // tq,),
        in_specs=[row(n_in), row(5 * LANES)],
        out_specs=[row(hd), row(2 * gd), row(n_ih * idx_dim), row(idx_dim), row(LANES), row(n_ih)],
        out_shape=[jax.ShapeDtypeStruct((m, hd), q_dtype), jax.ShapeDtypeStruct((m, 2 * gd), F32),
                   jax.ShapeDtypeStruct((m, n_ih * idx_dim), q_dtype),
                   jax.ShapeDtypeStruct((m, idx_dim), F32), jax.ShapeDtypeStruct((m, LANES), BF16),
                   jax.ShapeDtypeStruct((m, n_ih), F32)],
        compiler_params=_params(1),
        name="dsa_prep",
    )(proj, tab)


def _softmax_parts(s, valid, sink=None):
    sm = jnp.where(valid, s, MASK_NEG)
    m = jnp.max(sm, axis=-1, keepdims=True)
    if sink is not None:
        m = jnp.maximum(m, sink)
    m = jnp.where(m < 0.5 * MASK_NEG, 0.0, m)
    e = jnp.exp(sm - m)
    den = jnp.sum(e, axis=-1, keepdims=True)
    if sink is not None:
        den = den + jnp.exp(sink - m)
    return e, jnp.maximum(den, 1e-30)


def _count(mask):
    return jnp.sum(jnp.where(mask, 1.0, 0.0), axis=-1, keepdims=True)


def _topk_mask(score, k):
    w = score.shape[-1]
    bits = lax.bitcast_convert_type(score, I32)
    key = jnp.where(bits < 0, bits ^ 0x7FFFFFFF, bits)
    kf = float(k)
    t0 = jnp.where(_count(key >= 0) >= kf, 0, INT_MIN).astype(I32)

    def value_bit(i, t):
        cand = t | jnp.left_shift(jnp.int32(1), 30 - i)
        return jnp.where(_count(key >= cand) >= kf, cand, t)

    t = lax.fori_loop(0, 31, value_bit, t0)
    above = key > t
    tie = key == t
    need = kf - _count(above)
    idx = _iota(score.shape, score.ndim - 1)
    nb = w.bit_length()

    def index_bit(i, lim):
        cand = lim | jnp.left_shift(jnp.int32(1), nb - 1 - i)
        return jnp.where(_count(tie & (idx < cand)) <= need, cand, lim)

    lim = lax.fori_loop(0, nb, index_bit, jnp.zeros_like(t0))
    return above | (tie & (idx < lim))


def _pool_rows(x, alpha, cmp_blk):
    n = x.shape[0] // cmp_blk
    return jnp.sum(x.reshape(n, cmp_blk, x.shape[1]) * alpha[None], axis=1)


def _pool_prompt_kernel(rows_ref, alpha_ref, o_ref, *, n_cmp, cmp_blk):
    pooled = _pool_rows(rows_ref[0:n_cmp * cmp_blk, :], alpha_ref[...], cmp_blk)
    o_ref[0:n_cmp, :] = pooled
    if o_ref.shape[0] > n_cmp:
        o_ref[n_cmp:, :] = jnp.zeros((o_ref.shape[0] - n_cmp, o_ref.shape[1]), F32)


def pool_prompt(rows, alpha, n_cmp, n_cmp_pad, cmp_blk):
    b, t, wide = rows.shape
    half = wide // 2
    return pl.pallas_call(
        functools.partial(_pool_prompt_kernel, n_cmp=n_cmp, cmp_blk=cmp_blk),
        grid=(b,),
        in_specs=[pl.BlockSpec((None, t, half), lambda i: (i, 0, 0)),
                  pl.BlockSpec((cmp_blk, half), lambda i: (0, 0))],
        out_specs=pl.BlockSpec((None, n_cmp_pad, half), lambda i: (i, 0, 0)),
        out_shape=jax.ShapeDtypeStruct((b, n_cmp_pad, half), F32),
        compiler_params=_params(1),
        name="nsa_pool_prompt",
    )(rows, alpha)


def _pool_pages_kernel(pt_ref, *refs, n_pp, cmp_blk):
    del pt_ref
    pages, alpha_ref, o_ref = refs[:n_pp], refs[n_pp], refs[n_pp + 1]
    page, n_s, n_g, dh = pages[0].shape
    per_page = page // cmp_blk
    for p in range(n_pp):
        for s in range(n_s):
            for g in range(n_g):
                col = (s * n_g + g) * dh
                o_ref[p * per_page:(p + 1) * per_page, col:col + dh] = _pool_rows(
                    pages[p][:, s, g, :], alpha_ref[:, col:col + dh], cmp_blk)


def _page_specs(n_pp, layer, block, col):
    tail = (0, col) + (0,) * (len(block) - 2)

    def spec(j):
        return pl.BlockSpec((None, None) + block, lambda b, s, pt: (layer, pt[b, s * n_pp + j]) + tail)
    return [spec(j) for j in range(n_pp)]


def pool_pages(cache, layer, page_table, alpha, cmp_blk):
    _, _, page, _, n_g, dh = cache.shape
    b, n_pages = page_table.shape
    half = 2 * n_g * dh
    n_pp = min(NSA_PAGES_PER_STEP, n_pages)
    assert n_pages % n_pp == 0 and page % cmp_blk == 0
    per_step = n_pp * page // cmp_blk
    return pl.pallas_call(
        functools.partial(_pool_pages_kernel, n_pp=n_pp, cmp_blk=cmp_blk),
        grid_spec=pltpu.PrefetchScalarGridSpec(
            num_scalar_prefetch=1,
            grid=(b, n_pages // n_pp),
            in_specs=_page_specs(n_pp, layer, (page, 2, n_g, dh), 0)
            + [pl.BlockSpec((cmp_blk, half), lambda bi, s, pt: (0, 0))],
            out_specs=pl.BlockSpec((None, per_step, half), lambda bi, s, pt: (bi, s, 0)),
        ),
        out_shape=jax.ShapeDtypeStruct((b, n_pages * page // cmp_blk, half), F32),
        compiler_params=_params(2, 40),
        name="nsa_pool_pages",
    )(page_table, *([cache] * n_pp), alpha)


def _nsa_member(imp, q_pos, n_slc, n_sel, width):
    pair = imp + pltpu.roll(imp, imp.shape[1] - 1, 1)
    if width > imp.shape[1]:
        pair = jnp.concatenate([pair, jnp.zeros((imp.shape[0], width - imp.shape[1]), F32)], axis=1)
    c = _iota((1, width), 1)
    j = c >> 1
    sel_shift = _log2(NSA_SEL_BLOCK)
    visible = ((c & 1) == 0) & (j < n_slc) & ((j << sel_shift) <= q_pos)
    cur = q_pos >> sel_shift
    forced = (j == 0) | (j == cur) | (j == cur - 1)
    score = jnp.where(visible, pair + jnp.where(forced, NSA_FORCED_BONUS, 0.0), -jnp.inf)
    return _topk_mask(score, n_sel) & visible


def _expand_blocks(member, key0, n_keys):
    w = member.shape[1]
    c = _iota((w, n_keys), 0)
    k = key0 + _iota((w, n_keys), 1)
    e = (((c & 1) == 0) & ((c >> 1) == (k >> _log2(NSA_SEL_BLOCK)))).astype(BF16)
    return _dot(member.astype(BF16), e) > 0.5


def _flash_chunk(qg, k, v, valid, carry, r, scale):
    m, l, acc = carry
    tq = m.shape[1]
    s = (_dot_nt(qg, k) * scale).reshape(r, tq, k.shape[0])
    sm = jnp.where(valid, s, MASK_NEG)
    m_new = jnp.maximum(m, jnp.max(sm, axis=-1, keepdims=True))
    alpha = jnp.exp(m - m_new)
    e = jnp.where(valid, jnp.exp(sm - m_new), 0.0)
    l = alpha * l + jnp.sum(e, axis=-1, keepdims=True)
    acc = alpha.reshape(r * tq, 1) * acc + _dot(e.reshape(r * tq, k.shape[0]).astype(BF16), v)
    return m_new, l, acc


def _flash_init(r, tq, dh):
    return (jnp.full((r, tq, 1), MASK_NEG, F32), jnp.zeros((r, tq, 1), F32), jnp.zeros((r * tq, dh), F32))


def _flash_out(carry):
    _, l, acc = carry
    return acc / jnp.maximum(l, 1e-30).reshape(acc.shape[0], 1)


def _nsa_prompt_kernel(q_ref, qr_ref, gate_ref, pooled_ref, wk_ref, wv_ref, sel_ref, win_ref,
                       o_ref, sel_bf, win_bf, cmp_bf, *, t, tq, n_h, n_g, n_cmp, n_slc, n_sel, cmp_blk, chunk):
    dh = LANES
    r = n_h // n_g
    gd = n_g * dh
    scale = dh ** -0.5
    w = pooled_ref.shape[0]
    i = pl.program_id(1)

    @pl.when(i == 0)
    def _():
        sel_bf[...] = sel_ref[...].astype(BF16)
        win_bf[...] = win_ref[...].astype(BF16)
        wk = wk_ref[...].astype(BF16)
        wv = wv_ref[...].astype(BF16)
        for g in range(n_g):
            cmp_bf[:, g * dh:(g + 1) * dh] = _dot(pooled_ref[:, g * dh:(g + 1) * dh].astype(BF16), wk).astype(BF16)
            cmp_bf[:, gd + g * dh:gd + (g + 1) * dh] = _dot(
                pooled_ref[:, gd + g * dh:gd + (g + 1) * dh].astype(BF16), wv).astype(BF16)

    q_pos = i * tq + _iota((tq, 1), 0)
    c = _iota((1, w), 1)
    cmp_valid = (((c + 1) * cmp_blk - 1 <= q_pos) & (c < n_cmp))[None]
    band = min(t, NSA_WINDOW + tq)
    start = pl.multiple_of(jnp.maximum(i * tq + tq - band, 0), tq)
    rel = q_pos - (start + _iota((1, band), 1))
    win_valid = ((rel >= 0) & (rel < NSA_WINDOW))[None]
    gates = gate_ref[...]
    n_chunks = (i * tq + tq + chunk - 1) // chunk

    for g in range(n_g):
        qg = _stack_heads(q_ref, g, r)
        qrg = _stack_heads(qr_ref, g, r)
        s = (_dot_nt(qg, cmp_bf[:, g * dh:(g + 1) * dh]) * scale).reshape(r, tq, w)
        e, den = _softmax_parts(s, cmp_valid)
        p = e / den
        o_c = _dot(p.reshape(r * tq, w).astype(BF16), cmp_bf[:, gd + g * dh:gd + (g + 1) * dh])
        member = _nsa_member(jnp.sum(p, axis=0), q_pos, n_slc, n_sel, w)

        def sel_chunk(ci, carry):
            k0 = pl.multiple_of(ci * chunk, chunk)
            valid = (_expand_blocks(member, k0, chunk) & (k0 + _iota((1, chunk), 1) <= q_pos))[None]
            return _flash_chunk(qrg, sel_bf[pl.ds(k0, chunk), g * dh:(g + 1) * dh],
                                sel_bf[pl.ds(k0, chunk), gd + g * dh:gd + (g + 1) * dh], valid, carry, r, scale)

        o_s = _flash_out(lax.fori_loop(0, n_chunks, sel_chunk, _flash_init(r, tq, dh)))
        s = (_dot_nt(qrg, win_bf[pl.ds(start, band), g * dh:(g + 1) * dh]) * scale).reshape(r, tq, band)
        e, den = _softmax_parts(s, win_valid)
        o_w = _dot(e.reshape(r * tq, band).astype(BF16),
                   win_bf[pl.ds(start, band), gd + g * dh:gd + (g + 1) * dh]) / den.reshape(r * tq, 1)
        for rr in range(r):
            h = g * r + rr
            rows = slice(rr * tq, (rr + 1) * tq)
            o = (gates[:, h:h + 1] * o_c[rows] + gates[:, n_h + h:n_h + h + 1] * o_s[rows]
                 + gates[:, 2 * n_h + h:2 * n_h + h + 1] * o_w[rows])
            o_ref[:, h * dh:(h + 1) * dh] = o.astype(o_ref.dtype)


def nsa_prompt_attention(q, qr, gates, pooled, wk, wv, rows, win, *, b, t, n_h, n_g, n_cmp, cmp_blk):
    dh = LANES
    hd, gd = n_h * dh, n_g * dh
    tq = min(Q_TILE, t)
    nq = t // tq
    n_slc = -(-t // NSA_SEL_BLOCK)
    n_sel = min(NSA_N_SEL, n_slc)
    chunk = min(KEY_CHUNK, t)
    assert n_cmp % 2 == 0 and 2 * n_slc <= pooled.shape[1] and NSA_SEL_BLOCK == 2 * cmp_blk and t % chunk == 0
    kern = functools.partial(_nsa_prompt_kernel, t=t, tq=tq, n_h=n_h, n_g=n_g, n_cmp=n_cmp,
                             n_slc=n_slc, n_sel=n_sel, cmp_blk=cmp_blk, chunk=chunk)
    qspec = pl.BlockSpec((tq, hd), lambda bi, i: (bi * nq + i, 0))
    return pl.pallas_call(
        kern,
        grid=(b, nq),
        in_specs=[qspec, qspec,
                  pl.BlockSpec((tq, 3 * n_h), lambda bi, i: (bi * nq + i, 0)),
                  pl.BlockSpec((None,) + pooled.shape[1:], lambda bi, i: (bi, 0, 0)),
                  pl.BlockSpec((dh, dh), lambda bi, i: (0, 0)),
                  pl.BlockSpec((dh, dh), lambda bi, i: (0, 0)),
                  pl.BlockSpec((None, t, 2 * gd), lambda bi, i: (bi, 0, 1)),
                  pl.BlockSpec((None, t, 2 * gd), lambda bi, i: (bi, 0, 0))],
        out_specs=qspec,
        out_shape=jax.ShapeDtypeStruct((b * t, hd), BF16),
        scratch_shapes=[pltpu.VMEM((t, 2 * gd), BF16), pltpu.VMEM((t, 2 * gd), BF16),
                        pltpu.VMEM((pooled.shape[1], 2 * gd), BF16)],
        compiler_params=_params(2, 48),
        name="nsa_prompt_attention",
    )(q, qr, gates, pooled, wk, wv, rows, win)


def _swa_prompt_kernel(qr_ref, rows_ref, sink_ref, o_ref, *, t, tq, n_h, n_g):
    dh = LANES
    r = n_h // n_g
    gd = n_g * dh
    scale = dh ** -0.5
    i = pl.program_id(1)
    q_pos = i * tq + _iota((tq, 1), 0)
    band = min(t, SWA_WINDOW + tq)
    start = pl.multiple_of(jnp.maximum(i * tq + tq - band, 0), tq)
    rel = q_pos - (start + _iota((1, band), 1))
    valid = (rel >= 0) & (rel < SWA_WINDOW)
    sinks = sink_ref[...]
    for g in range(n_g):
        k = rows_ref[pl.ds(start, band), g * dh:(g + 1) * dh].astype(BF16)
        v = rows_ref[pl.ds(start, band), gd + g * dh:gd + (g + 1) * dh].astype(BF16)
        for rr in range(r):
            h = g * r + rr
            e, den = _softmax_parts(_dot_nt(qr_ref[:, h * dh:(h + 1) * dh], k) * scale, valid,
                                    sinks[:, h:h + 1])
            o_ref[:, h * dh:(h + 1) * dh] = (_dot(e.astype(BF16), v) / den).astype(o_ref.dtype)


def swa_prompt_attention(qr, rows, sinks, *, b, t, n_h, n_g):
    dh = LANES
    hd, gd = n_h * dh, n_g * dh
    tq = min(Q_TILE, t)
    nq = t // tq
    qspec = pl.BlockSpec((tq, hd), lambda bi, i: (bi * nq + i, 0))
    return pl.pallas_call(
        functools.partial(_swa_prompt_kernel, t=t, tq=tq, n_h=n_h, n_g=n_g),
        grid=(b, nq),
        in_specs=[qspec,
                  pl.BlockSpec((None, t, 2 * gd), lambda bi, i: (bi, 0, 0)),
                  pl.BlockSpec((1, n_h), lambda bi, i: (0, 0))],
        out_specs=qspec,
        out_shape=jax.ShapeDtypeStruct((b * t, hd), BF16),
        compiler_params=_params(2, 48),
        name="swa_prompt_attention",
    )(qr, rows, sinks.reshape(1, n_h))


def _indexer_scores(qi, ki2, wt, n_ih, idx_dim):
    per = LANES // idx_dim
    lane = _iota((1, LANES), 1)
    acc = None
    for c in range(n_ih // per):
        pair = qi[:, c * LANES:(c + 1) * LANES]
        for u in range(per):
            h = c * per + u
            qh = jnp.where((lane >= u * idx_dim) & (lane < (u + 1) * idx_dim), pair, jnp.zeros_like(pair))
            sc = _dot_nt(qh, ki2) * (idx_dim ** -0.5)
            term = jnp.maximum(sc, 0.0) * wt[:, h:h + 1]
            acc = term if acc is None else acc + term
    return acc


def _dsa_prompt_kernel(qr_ref, qi_ref, wt_ref, ki_ref, rows_ref, o_ref, kv_bf, member_ref, *, t, tq, n_h, n_g,
                       n_ih, idx_dim, n_keep, chunk):
    dh = LANES
    r = n_h // n_g
    gd = n_g * dh
    scale = dh ** -0.5
    i = pl.program_id(1)

    @pl.when(i == 0)
    def _():
        kv_bf[...] = rows_ref[...].astype(BF16)

    q_pos = i * tq + _iota((tq, 1), 0)
    causal = _iota((1, t), 1) <= q_pos
    score = _indexer_scores(qi_ref[...], ki_ref[...], wt_ref[...], n_ih, idx_dim)
    score = jnp.where(causal, score, -jnp.inf)
    member_ref[...] = jnp.where(_topk_mask(score, n_keep) & causal, 1.0, 0.0)
    n_chunks = (i * tq + tq + chunk - 1) // chunk
    for g in range(n_g):
        qg = _stack_heads(qr_ref, g, r)

        def key_chunk(ci, carry):
            k0 = pl.multiple_of(ci * chunk, chunk)
            valid = (member_ref[:, pl.ds(k0, chunk)] > 0.5)[None]
            return _flash_chunk(qg, kv_bf[pl.ds(k0, chunk), g * dh:(g + 1) * dh],
                                kv_bf[pl.ds(k0, chunk), gd + g * dh:gd + (g + 1) * dh], valid, carry, r, scale)

        o = _flash_out(lax.fori_loop(0, n_chunks, key_chunk, _flash_init(r, tq, dh)))
        for rr in range(r):
            h = g * r + rr
            o_ref[:, h * dh:(h + 1) * dh] = o[rr * tq:(rr + 1) * tq].astype(o_ref.dtype)


def dsa_prompt_attention(qr, qi, wt, ki2, rows, *, b, t, n_h, n_g, n_ih, idx_dim):
    dh = LANES
    hd, gd = n_h * dh, n_g * dh
    tq = min(Q_TILE, t)
    nq = t // tq
    n_keep = min(DSA_TOPK, t // 4)
    chunk = min(KEY_CHUNK, t)
    assert t % chunk == 0
    qrow = lambda wdt: pl.BlockSpec((tq, wdt), lambda bi, i: (bi * nq + i, 0))
    return pl.pallas_call(
        functools.partial(_dsa_prompt_kernel, t=t, tq=tq, n_h=n_h, n_g=n_g, n_ih=n_ih,
                          idx_dim=idx_dim, n_keep=n_keep, chunk=chunk),
        grid=(b, nq),
        in_specs=[qrow(hd), qrow(n_ih * idx_dim), qrow(n_ih),
                  pl.BlockSpec((None, t, LANES), lambda bi, i: (bi, 0, 0)),
                  pl.BlockSpec((None, t, 2 * gd), lambda bi, i: (bi, 0, 0))],
        out_specs=qrow(hd),
        out_shape=jax.ShapeDtypeStruct((b * t, hd), BF16),
        scratch_shapes=[pltpu.VMEM((t, 2 * gd), BF16), pltpu.VMEM((tq, t), F32)],
        compiler_params=_params(2, 48),
        name="dsa_prompt_attention",
    )(qr, qi, wt, ki2, rows)


def _stack_heads(x_ref, g, r):
    return jnp.concatenate([x_ref[:, (g * r + rr) * LANES:(g * r + rr + 1) * LANES] for rr in range(r)], axis=0)


def _tile_rows(mask, r):
    return jnp.concatenate([jnp.where(mask, 1.0, 0.0)] * r, axis=0) > 0.5


def _online_update(m_ref, l_ref, acc_ref, g, s, valid, v):
    sm = jnp.where(valid, s, MASK_NEG)
    m_old = m_ref[g]
    m_new = jnp.maximum(m_old, jnp.max(sm, axis=-1, keepdims=True))
    alpha = jnp.exp(m_old - m_new)
    e = jnp.where(valid, jnp.exp(sm - m_new), 0.0)
    l_ref[g] = alpha * l_ref[g] + jnp.sum(e, axis=-1, keepdims=True)
    acc_ref[g] = alpha * acc_ref[g] + _dot(e.astype(BF16), v)
    m_ref[g] = m_new


def _online_init(m_ref, l_ref, acc_ref):
    m_ref[...] = jnp.full(m_ref.shape, MASK_NEG, F32)
    l_ref[...] = jnp.zeros(l_ref.shape, F32)
    acc_ref[...] = jnp.zeros(acc_ref.shape, F32)


def _cat_pages(pages, stream, g):
    return jnp.concatenate([p[:, stream, g, :] for p in pages], axis=0).astype(BF16)


def _nsa_sample_kernel(pt_ref, *refs, n_pp, n_steps, td, n_h, n_g, n_cmp, n_slc, n_sel, cmp_blk, past_len,
                       width):
    del pt_ref
    pages = refs[:n_pp]
    (q_ref, qr_ref, gate_ref, pooled_ref, wk_ref, wv_ref, new_ref, buf_ref, nwin_ref,
     o_ref, member_ref, oc_ref, m_ref, l_ref, acc_ref) = refs[n_pp:]
    dh = LANES
    r = n_h // n_g
    gd = n_g * dh
    scale = dh ** -0.5
    page = pages[0].shape[0]
    s_id = pl.program_id(1)
    tok = _iota((td, 1), 0)
    q_pos = past_len + tok

    @pl.when(s_id == 0)
    def _():
        _online_init(m_ref, l_ref, acc_ref)
        w = pooled_ref.shape[0]
        c = _iota((1, w), 1)
        cmp_valid = _tile_rows(((c + 1) * cmp_blk - 1 <= q_pos) & (c < n_cmp), r)
        wk = wk_ref[...].astype(BF16)
        wv = wv_ref[...].astype(BF16)
        for g in range(n_g):
            kc = _dot(pooled_ref[:, g * dh:(g + 1) * dh].astype(BF16), wk).astype(BF16)
            vc = _dot(pooled_ref[:, gd + g * dh:gd + (g + 1) * dh].astype(BF16), wv).astype(BF16)
            qg = _stack_heads(q_ref, g, r).astype(BF16)
            e, den = _softmax_parts(_dot_nt(qg, kc) * scale, cmp_valid)
            p = e / den
            oc_ref[g] = _dot(p.astype(BF16), vc)
            imp = jnp.sum(p.reshape(r, td, w), axis=0)
            member = _nsa_member(imp, q_pos, n_slc, n_sel, width)
            member_ref[g] = jnp.where(member, 1.0, 0.0)

    lanes = 2 * n_pp * page // NSA_SEL_BLOCK
    lane0 = pl.multiple_of(s_id * lanes, LANES)
    for g in range(n_g):
        qg = _stack_heads(qr_ref, g, r).astype(BF16)
        valid = _tile_rows(_expand_blocks(member_ref[g, :, pl.ds(lane0, lanes)] > 0.5, 0, n_pp * page), r)
        _online_update(m_ref, l_ref, acc_ref, g, _dot_nt(qg, _cat_pages(pages, 0, g)) * scale, valid,
                       _cat_pages(pages, 1, g))

    @pl.when(s_id == n_steps - 1)
    def _():
        gates = gate_ref[...]
        new_pos = past_len + _iota((1, td), 1)
        w_len = buf_ref.shape[0]
        buf_pos = past_len - w_len + _iota((1, w_len), 1)
        rel = jnp.concatenate([q_pos - buf_pos, q_pos - new_pos], axis=1)
        win_valid = _tile_rows((rel >= 0) & (rel < NSA_WINDOW), r)
        for g in range(n_g):
            qg = _stack_heads(qr_ref, g, r).astype(BF16)
            new_sel = member_ref[g][:, 2 * (n_slc - 1):2 * (n_slc - 1) + 1] > 0.5
            valid = _tile_rows(new_sel & (new_pos <= q_pos), r)
            k = new_ref[:, (2 * n_g + g) * dh:(2 * n_g + g + 1) * dh].astype(BF16)
            v = new_ref[:, (3 * n_g + g) * dh:(3 * n_g + g + 1) * dh].astype(BF16)
            _online_update(m_ref, l_ref, acc_ref, g, _dot_nt(qg, k) * scale, valid, v)
            o_s = acc_ref[g] / jnp.maximum(l_ref[g], 1e-30)
            k = jnp.concatenate([buf_ref[:, 0, g, :], nwin_ref[:, g * dh:(g + 1) * dh]], axis=0).astype(BF16)
            v = jnp.concatenate([buf_ref[:, 1, g, :], nwin_ref[:, gd + g * dh:gd + (g + 1) * dh]],
                                axis=0).astype(BF16)
            e, den = _softmax_parts(_dot_nt(qg, k) * scale, win_valid)
            o_w = _dot(e.astype(BF16), v) / den
            o_c = oc_ref[g]
            for rr in range(r):
                h = g * r + rr
                rows = slice(rr * td, (rr + 1) * td)
                o = (gates[:, h:h + 1] * o_c[rows] + gates[:, n_h + h:n_h + h + 1] * o_s[rows]
                     + gates[:, 2 * n_h + h:2 * n_h + h + 1] * o_w[rows])
                o_ref[:, h * dh:(h + 1) * dh] = o.astype(o_ref.dtype)


def nsa_sample_attention(q, qr, gates, pooled, wk, wv, new_rows, win_buf, new_win, cache, layer, page_table,
                         *, n_h, n_g, cmp_blk):
    dh = LANES
    b, td, hd = q.shape
    gd = n_g * dh
    r = n_h // n_g
    _, n_pages = page_table.shape
    page = cache.shape[2]
    past_len = n_pages * page
    seq_len = past_len + td
    n_cmp = seq_len // cmp_blk
    n_slc = -(-seq_len // NSA_SEL_BLOCK)
    n_sel = min(NSA_N_SEL, n_slc)
    n_pp = min(NSA_PAGES_PER_STEP, n_pages)
    n_steps = n_pages // n_pp
    assert past_len % NSA_SEL_BLOCK == 0 and td <= cmp_blk and n_cmp == pooled.shape[1]
    assert n_cmp % LANES == 0 and n_pages % n_pp == 0 and NSA_SEL_BLOCK == 2 * cmp_blk
    assert (2 * n_pp * page // NSA_SEL_BLOCK) % LANES == 0
    width = n_cmp + LANES
    full = lambda shape: pl.BlockSpec((None,) + shape, lambda bi, s, pt: (bi, 0, 0))
    const = lambda shape: pl.BlockSpec(shape, lambda bi, s, pt: (0, 0))
    kern = functools.partial(_nsa_sample_kernel, n_pp=n_pp, n_steps=n_steps, td=td, n_h=n_h, n_g=n_g,
                             n_cmp=n_cmp, n_slc=n_slc, n_sel=n_sel, cmp_blk=cmp_blk, past_len=past_len,
                             width=width)
    return pl.pallas_call(
        kern,
        grid_spec=pltpu.PrefetchScalarGridSpec(
            num_scalar_prefetch=1,
            grid=(b, n_steps),
            in_specs=_page_specs(n_pp, layer, (page, 2, n_g, dh), 1)
            + [full((td, hd)), full((td, hd)), full((td, 3 * n_h)), full((n_cmp, 2 * gd)),
               const((dh, dh)), const((dh, dh)), full((td, 4 * gd)),
               pl.BlockSpec((None, None) + win_buf.shape[2:], lambda bi, s, pt: (layer, bi, 0, 0, 0, 0)),
               full((td, 2 * gd))],
            out_specs=full((td, hd)),
            scratch_shapes=[pltpu.VMEM((n_g, td, width), F32), pltpu.VMEM((n_g, r * td, dh), F32),
                            pltpu.VMEM((n_g, r * td, 1), F32), pltpu.VMEM((n_g, r * td, 1), F32),
                            pltpu.VMEM((n_g, r * td, dh), F32)],
        ),
        out_shape=jax.ShapeDtypeStruct((b, td, hd), BF16),
        compiler_params=_params(2, 48),
        name="nsa_sample_attention",
    )(page_table, *([cache] * n_pp), q, qr, gates, pooled, wk, wv, new_rows, win_buf, new_win)


def _swa_sample_kernel(qr_ref, buf_ref, new_ref, sink_ref, o_ref, *, td, n_h, n_g, past_len):
    dh = LANES
    r = n_h // n_g
    gd = n_g * dh
    scale = dh ** -0.5
    q_pos = past_len + _iota((td, 1), 0)
    w_len = buf_ref.shape[0]
    k_pos = jnp.concatenate([past_len - w_len + _iota((1, w_len), 1), past_len + _iota((1, td), 1)], axis=1)
    rel = q_pos - k_pos
    valid = _tile_rows((rel >= 0) & (rel < SWA_WINDOW) & (k_pos >= 0), r)
    sinks = sink_ref[...]
    for g in range(n_g):
        qg = _stack_heads(qr_ref, g, r).astype(BF16)
        k = jnp.concatenate([buf_ref[:, 0, g, :], new_ref[:, g * dh:(g + 1) * dh]], axis=0).astype(BF16)
        v = jnp.concatenate([buf_ref[:, 1, g, :], new_ref[:, gd + g * dh:gd + (g + 1) * dh]], axis=0).astype(BF16)
        sink = jnp.concatenate([jnp.broadcast_to(sinks[:, g * r + rr:g * r + rr + 1], (td, 1))
                                for rr in range(r)], axis=0)
        e, den = _softmax_parts(_dot_nt(qg, k) * scale, valid, sink)
        o = _dot(e.astype(BF16), v) / den
        for rr in range(r):
            h = g * r + rr
            o_ref[:, h * dh:(h + 1) * dh] = o[rr * td:(rr + 1) * td].astype(o_ref.dtype)


def swa_sample_attention(qr, buf, layer, new_rows, sinks, *, n_h, n_g, past_len):
    b, td, hd = qr.shape
    full = lambda shape: pl.BlockSpec((None,) + shape, lambda bi: (bi, 0, 0))
    return pl.pallas_call(
        functools.partial(_swa_sample_kernel, td=td, n_h=n_h, n_g=n_g, past_len=past_len),
        grid=(b,),
        in_specs=[full((td, hd)),
                  pl.BlockSpec((None, None) + buf.shape[2:], lambda bi: (layer, bi, 0, 0, 0, 0)),
                  full(new_rows.shape[1:]),
                  pl.BlockSpec((1, n_h), lambda bi: (0, 0))],
        out_specs=full((td, hd)),
        out_shape=jax.ShapeDtypeStruct((b, td, hd), BF16),
        compiler_params=_params(1),
        name="swa_sample_attention",
    )(qr, buf, new_rows, sinks.reshape(1, n_h))


def _dsa_select_kernel(pt_ref, *refs, n_pp, n_steps, td, n_ih, idx_dim, n_keep):
    del pt_ref
    pages = refs[:n_pp]
    qi_ref, wt_ref, kin_ref, o_ref, score_ref = refs[n_pp:]
    page = pages[0].shape[0]
    s_id = pl.program_id(1)
    chunk = n_pp * page
    n_past = n_steps * chunk

    def scores(ki):
        sc = _dot_nt(qi_ref[...], ki.astype(BF16)) * (idx_dim ** -0.5)
        term = jnp.maximum(sc, 0.0) * wt_ref[...]
        return jnp.sum(term.reshape(n_ih, td, ki.shape[0]), axis=0)

    ki = jnp.concatenate([p[...] for p in pages], axis=0)
    score_ref[:, pl.ds(pl.multiple_of(s_id * chunk, LANES), chunk)] = scores(ki)

    @pl.when(s_id == n_steps - 1)
    def _():
        tok = _iota((td, 1), 0)
        lane = _iota((1, LANES), 1)
        score_ref[:, n_past:] = jnp.where((lane <= tok) & (lane < td), scores(kin_ref[...]), -jnp.inf)
        score = score_ref[...]
        member = _topk_mask(score, n_keep) & (score > -jnp.inf)
        o_ref[...] = jnp.where(member, 1.0, 0.0).astype(o_ref.dtype)


def dsa_sample_select(qi, wt, ki_new, cache_idx, layer, page_table, *, n_ih, idx_dim):
    b, n_pages = page_table.shape
    page = cache_idx.shape[2]
    td = qi.shape[1] // n_ih
    n_pp = min(IDX_PAGES_PER_STEP, n_pages)
    assert n_pages % n_pp == 0
    n_steps = n_pages // n_pp
    n_keys = n_pages * page + LANES
    n_keep = min(DSA_TOPK, (n_pages * page + td) // 4)
    full = lambda shape: pl.BlockSpec((None,) + shape, lambda bi, s, pt: (bi, 0, 0))
    return pl.pallas_call(
        functools.partial(_dsa_select_kernel, n_pp=n_pp, n_steps=n_steps, td=td, n_ih=n_ih, idx_dim=idx_dim,
                          n_keep=n_keep),
        grid_spec=pltpu.PrefetchScalarGridSpec(
            num_scalar_prefetch=1,
            grid=(b, n_steps),
            in_specs=_page_specs(n_pp, layer, (page, idx_dim), 0)
            + [full(qi.shape[1:]), full(wt.shape[1:]), full(ki_new.shape[1:])],
            out_specs=full((td, n_keys)),
            scratch_shapes=[pltpu.VMEM((td, n_keys), F32)],
        ),
        out_shape=jax.ShapeDtypeStruct((b, td, n_keys), BF16),
        compiler_params=_params(2),
        name="dsa_sample_select",
    )(page_table, *([cache_idx] * n_pp), qi, wt, ki_new)


def _dsa_sample_kernel(pt_ref, *refs, n_pp, n_steps, td, n_h, n_g):
    del pt_ref
    pages = refs[:n_pp]
    qr_ref, mem_ref, tail_ref, new_ref, o_ref, m_ref, l_ref, acc_ref = refs[n_pp:]
    dh = LANES
    r = n_h // n_g
    gd = n_g * dh
    scale = dh ** -0.5
    s_id = pl.program_id(1)

    @pl.when(s_id == 0)
    def _():
        _online_init(m_ref, l_ref, acc_ref)

    valid = _tile_rows(mem_ref[...].astype(F32) > 0.5, r)
    for g in range(n_g):
        qg = _stack_heads(qr_ref, g, r).astype(BF16)
        _online_update(m_ref, l_ref, acc_ref, g, _dot_nt(qg, _cat_pages(pages, 0, g)) * scale, valid,
                       _cat_pages(pages, 1, g))

    @pl.when(s_id == n_steps - 1)
    def _():
        new_valid = _tile_rows(tail_ref[...].astype(F32) > 0.5, r)
        for g in range(n_g):
            qg = _stack_heads(qr_ref, g, r).astype(BF16)
            k = new_ref[:, g * dh:(g + 1) * dh].astype(BF16)
            v = new_ref[:, gd + g * dh:gd + (g + 1) * dh].astype(BF16)
            _online_update(m_ref, l_ref, acc_ref, g, _dot_nt(qg, k) * scale, new_valid, v)
            o = acc_ref[g] / jnp.maximum(l_ref[g], 1e-30)
            for rr in range(r):
                h = g * r + rr
                o_ref[:, h * dh:(h + 1) * dh] = o[rr * td:(rr + 1) * td].astype(o_ref.dtype)


def dsa_sample_attention(qr, member, new_rows, cache, layer, page_table, *, n_h, n_g):
    dh = LANES
    b, td, hd = qr.shape
    gd = n_g * dh
    r = n_h // n_g
    _, n_pages = page_table.shape
    page = cache.shape[2]
    n_pp = min(DSA_PAGES_PER_STEP, n_pages)
    assert n_pages % n_pp == 0
    n_steps = n_pages // n_pp
    chunk = n_pp * page
    full = lambda shape: pl.BlockSpec((None,) + shape, lambda bi, s, pt: (bi, 0, 0))
    return pl.pallas_call(
        functools.partial(_dsa_sample_kernel, n_pp=n_pp, n_steps=n_steps, td=td, n_h=n_h, n_g=n_g),
        grid_spec=pltpu.PrefetchScalarGridSpec(
            num_scalar_prefetch=1,
            grid=(b, n_steps),
            in_specs=_page_specs(n_pp, layer, (page, 2, n_g, dh), 0)
            + [full((td, hd)),
               pl.BlockSpec((None, td, chunk), lambda bi, s, pt: (bi, 0, s)),
               pl.BlockSpec((None, td, LANES), lambda bi, s, pt: (bi, 0, n_steps * chunk // LANES)),
               full(new_rows.shape[1:])],
            out_specs=full((td, hd)),
            scratch_shapes=[pltpu.VMEM((n_g, r * td, 1), F32), pltpu.VMEM((n_g, r * td, 1), F32),
                            pltpu.VMEM((n_g, r * td, dh), F32)],
        ),
        out_shape=jax.ShapeDtypeStruct((b, td, hd), BF16),
        compiler_params=_params(2, 48),
        name="dsa_sample_attention",
    )(page_table, *([cache] * n_pp), qr, member, member, new_rows)


def _pad_rows(x, n):
    return jnp.pad(x, ((0, 0), (0, n - x.shape[1]), (0, 0)))


def _run(x, pos, sample, weights, dims):
    (norm_mixer, norm_mlp, norm_final, w_mlp_up, w_mlp_down, nsa_w_in, nsa_w_out, nsa_cmp_pos_k,
     nsa_cmp_pos_v, nsa_cmp_wk, nsa_cmp_wv, swa_w_in, swa_w_out, swa_sinks, dsa_w_in, dsa_w_out) = weights
    n_h, g_nsa, g_swa, g_dsa, n_ih, idx_dim, cmp_blk = dims
    b, t, d = x.shape
    m = b * t
    dh = LANES
    hd = n_h * dh
    depth = norm_mixer.shape[0]
    q_dtype = BF16 if sample is None else F32
    tab = jnp.tile(rope_tables(pos, dh, idx_dim), (b, 1))
    x = x.reshape(m, d)
    nsa_rows, nsa_win, swa_win, dsa_rows, dsa_idx = [], [], [], [], []
    for i in range(depth):
        kind, j = i % 3, i // 3
        h = rmsnorm(x, norm_mixer[i], BF16)
        if kind == 0:
            gd = g_nsa * dh
            proj = matmul(h, nsa_w_in[j])
            q, qr, rows, win, gates = nsa_prep(proj, tab, n_h, g_nsa, q_dtype)
            alpha = jnp.concatenate([jnp.broadcast_to(nsa_cmp_pos_k[j][:, None], (cmp_blk, gd)),
                                     jnp.broadcast_to(nsa_cmp_pos_v[j][:, None], (cmp_blk, gd))], axis=1)
            if sample is None:
                n_cmp = t // cmp_blk
                n_cmp_pad = -(-n_cmp // LANES) * LANES
                pooled = pool_prompt(rows.reshape(b, t, 4 * gd), alpha, n_cmp, n_cmp_pad, cmp_blk)
                o = nsa_prompt_attention(q, qr, gates, pooled, nsa_cmp_wk[j], nsa_cmp_wv[j],
                                         rows.reshape(b, t, 4 * gd), win.reshape(b, t, 2 * gd),
                                         b=b, t=t, n_h=n_h, n_g=g_nsa, n_cmp=n_cmp, cmp_blk=cmp_blk)
                nsa_win.append(win.reshape(b, t, 2, g_nsa, dh)[:, -min(NSA_WINDOW, t):])
            else:
                cache = sample["cache_nsa"]
                buf = sample["state_nsa_win"][j]
                pooled = pool_pages(cache, j, sample["page_table"], alpha, cmp_blk)
                o = nsa_sample_attention(q.reshape(b, t, hd), qr.reshape(b, t, hd), gates.reshape(b, t, 3 * n_h),
                                         pooled, nsa_cmp_wk[j], nsa_cmp_wv[j], rows.reshape(b, t, 4 * gd),
                                         sample["state_nsa_win"], win.reshape(b, t, 2 * gd),
                                         cache, j, sample["page_table"], n_h=n_h, n_g=g_nsa, cmp_blk=cmp_blk)
                o = o.reshape(m, hd)
                nsa_win.append(jnp.concatenate([buf, win.reshape(b, t, 2, g_nsa, dh)], axis=1)[:, -buf.shape[1]:])
            nsa_rows.append(rows.reshape(b, t, 4, g_nsa, dh))
            w_out = nsa_w_out[j]
        elif kind == 1:
            gd = g_swa * dh
            proj = matmul(h, swa_w_in[j])
            qr, rows = swa_prep(proj, tab, n_h, g_swa, q_dtype)
            if sample is None:
                o = swa_prompt_attention(qr, rows.reshape(b, t, 2 * gd), swa_sinks[j], b=b, t=t, n_h=n_h,
                                         n_g=g_swa)
                swa_win.append(rows.reshape(b, t, 2, g_swa, dh)[:, -min(SWA_WINDOW, t):])
            else:
                buf = sample["state_swa"][j]
                o = swa_sample_attention(qr.reshape(b, t, hd), sample["state_swa"], j,
                                         rows.reshape(b, t, 2 * gd), swa_sinks[j], n_h=n_h, n_g=g_swa,
                                         past_len=sample["past_len"]).reshape(m, hd)
                swa_win.append(jnp.concatenate([buf, rows.reshape(b, t, 2, g_swa, dh)], axis=1)[:, -buf.shape[1]:])
            w_out = swa_w_out[j]
        else:
            gd = g_dsa * dh
            n_in = dsa_w_in.shape[-1]
            w_in = jnp.pad(dsa_w_in[j], ((0, 0), (0, -n_in % LANES)))
            proj = matmul(h, w_in)
            qr, rows, qi, ki, ki2, wt = dsa_prep(proj, tab, n_h, g_dsa, n_ih, idx_dim, q_dtype)
            if sample is None:
                o = dsa_prompt_attention(qr, qi, wt, ki2.reshape(b, t, LANES), rows.reshape(b, t, 2 * gd),
                                         b=b, t=t, n_h=n_h, n_g=g_dsa, n_ih=n_ih, idx_dim=idx_dim)
            else:
                qi_rows = qi.reshape(b, t, n_ih, idx_dim).transpose(0, 2, 1, 3).reshape(b, n_ih * t, idx_dim)
                wt_rows = wt.reshape(b, t, n_ih).transpose(0, 2, 1).reshape(b, n_ih * t, 1)
                member = dsa_sample_select(qi_rows.astype(BF16), wt_rows, _pad_rows(ki.reshape(b, t, idx_dim), LANES),
                                           sample["cache_dsa_idx"], j, sample["page_table"], n_ih=n_ih,
                                           idx_dim=idx_dim)
                o = dsa_sample_attention(qr.reshape(b, t, hd), member, _pad_rows(rows.reshape(b, t, 2 * gd), LANES),
                                         sample["cache_dsa"], j, sample["page_table"], n_h=n_h,
                                         n_g=g_dsa).reshape(m, hd)
            dsa_rows.append(rows.reshape(b, t, 2, g_dsa, dh))
            dsa_idx.append(ki.reshape(b, t, idx_dim))
            w_out = dsa_w_out[j]
        x = matmul(o, w_out, residual=x)
        h = rmsnorm(x, norm_mlp[i], BF16)
        u = matmul(h, w_mlp_up[i], act="relu2", out_dtype=BF16)
        x = matmul(u, w_mlp_down[i], residual=x)
    y = rmsnorm(x, norm_final, F32).reshape(b, t, d)
    return y, (jnp.stack(nsa_rows), jnp.stack(nsa_win), jnp.stack(swa_win), jnp.stack(dsa_rows),
               jnp.stack(dsa_idx))


def kernel(x_prompt, x_sample, cache_nsa, state_nsa_win, state_swa, cache_dsa, cache_dsa_idx, page_table, norm_mixer, norm_mlp, norm_final, w_mlp_up, w_mlp_down, nsa_w_in, nsa_w_out, nsa_cmp_pos_k, nsa_cmp_pos_v, nsa_cmp_wk, nsa_cmp_wv, swa_w_in, swa_w_out, swa_sinks, dsa_w_in, dsa_w_out):
    n_h = swa_sinks.shape[-1]
    dh = cache_nsa.shape[-1]
    assert dh == LANES and x_prompt.shape[-1] == n_h * dh
    g_nsa, g_swa, g_dsa = cache_nsa.shape[4], state_swa.shape[4], cache_dsa.shape[4]
    idx_dim = cache_dsa_idx.shape[-1]
    n_ih = (dsa_w_in.shape[-1] - n_h * dh - 2 * g_dsa * dh - idx_dim) // (idx_dim + 1)
    cmp_blk = nsa_cmp_pos_k.shape[-1]
    dims = (n_h, g_nsa, g_swa, g_dsa, n_ih, idx_dim, cmp_blk)
    weights = (norm_mixer, norm_mlp, norm_final, w_mlp_up, w_mlp_down, nsa_w_in, nsa_w_out, nsa_cmp_pos_k,
               nsa_cmp_pos_v, nsa_cmp_wk, nsa_cmp_wv, swa_w_in, swa_w_out, swa_sinks, dsa_w_in, dsa_w_out)
    past_len = page_table.shape[1] * cache_nsa.shape[2]
    sample = dict(cache_nsa=cache_nsa, state_nsa_win=state_nsa_win, state_swa=state_swa, cache_dsa=cache_dsa,
                  cache_dsa_idx=cache_dsa_idx, page_table=page_table, past_len=past_len)
    y_p, (p_nsa, p_nsa_win, p_swa, p_dsa, p_dsa_idx) = _run(
        x_prompt, jnp.arange(x_prompt.shape[1], dtype=jnp.int32), None, weights, dims)
    y_s, (s_nsa, s_nsa_win, s_swa, s_dsa, s_dsa_idx) = _run(
        x_sample, past_len + jnp.arange(x_sample.shape[1], dtype=jnp.int32), sample, weights, dims)
    return (y_p, y_s, p_nsa, s_nsa, p_nsa_win, s_nsa_win, p_swa, s_swa, p_dsa, s_dsa, p_dsa_idx, s_dsa_idx)
```
